```python
import math
import jax, jax.numpy as jnp
from jax import lax
import numpy as np

D_MODEL = 4096
BATCH = 2
SEQ = 8192
DEPTH = 2

HEAD_DIM = 64
RMS_EPS = 1e-6
ROPE_THETA = 10000.0
Q_BLOCK = 128
DA_HEADS = D_MODEL // 512
DA_QK_COLS = 2 * DA_HEADS * HEAD_DIM
DA_V_DIM = 2 * HEAD_DIM
DA_WIDTH = DA_HEADS * DA_V_DIM
RW_HEADS = D_MODEL // 256
RW_HEAD = HEAD_DIM
RW_WIDTH = RW_HEADS * RW_HEAD
RW_DECAY_LORA = 64
RW_AAA_LORA = 64
RW_GATE_LORA = 128
RW_SIZES = (RW_WIDTH, RW_WIDTH, RW_WIDTH, RW_DECAY_LORA, RW_AAA_LORA, RW_GATE_LORA)
RW_COLS = sum(RW_SIZES)
RW_GN_EPS = 64e-5
SSM_HEADS = D_MODEL // 128
SSM_HEAD = HEAD_DIM
SSM_WIDTH = SSM_HEADS * SSM_HEAD
SSM_GROUPS = 4
SSM_STATE = 128
SSM_CONV = 4
SSM_CHUNK = 256
SSM_XBC = SSM_WIDTH + 2 * SSM_GROUPS * SSM_STATE
SSM_NORM_EPS = 1e-5
N_BRANCH = 3
MIX_WIDTH = DA_WIDTH + RW_WIDTH + SSM_WIDTH
IN_SIZES = (N_BRANCH * D_MODEL, DA_QK_COLS, DA_QK_COLS, DA_WIDTH, RW_COLS, SSM_WIDTH, SSM_XBC, SSM_HEADS)
IN_COLS = sum(IN_SIZES)
PEER_HEADS = 8
PEER_KEYS = 128
PEER_EXPERTS = PEER_KEYS * PEER_KEYS
PEER_TOPK = 16
PEER_QDIM = 256
PEER_TOKEN_BLOCK = 2048

kernel_name = 'hybrid_diffattn_rwkv7_mamba2_peer'


def _split_last(x, sizes):
    offs = np.cumsum(np.array(sizes))[:-1].tolist()
    return jnp.split(x, offs, axis=-1)


def rms_norm(x, w, eps=RMS_EPS):
    xf = x.astype(jnp.float32)
    y = xf * lax.rsqrt(jnp.mean(xf * xf, axis=-1, keepdims=True) + eps)
    return (y * w.astype(jnp.float32)).astype(x.dtype)


def rotary(x, cos, sin):
    x1, x2 = jnp.split(x.astype(jnp.float32), 2, axis=-1)
    c = cos[None, :, None, :]
    s = sin[None, :, None, :]
    return jnp.concatenate([x1 * c - x2 * s, x2 * c + x1 * s], axis=-1).astype(x.dtype)


def token_shift(x):
    return jnp.pad(x, ((0, 0), (1, 0), (0, 0)))[:, :-1]


def diff_attention(q, k, v, cos, sin, q_norm_w, k_norm_w, lam_vecs, sub_norm_w, lambda_init):
    Bsz, S = q.shape[0], q.shape[1]
    q = rotary(rms_norm(q, q_norm_w), cos, sin) * (HEAD_DIM ** -0.5)
    k = rotary(rms_norm(k, k_norm_w), cos, sin)
    lv = lam_vecs.astype(jnp.float32)
    lam = jnp.exp(jnp.sum(lv[0] * lv[1])) - jnp.exp(jnp.sum(lv[2] * lv[3])) + lambda_init
    key_pos = jnp.arange(S)

    def block(i):
        start = i * Q_BLOCK
        qb = lax.dynamic_slice_in_dim(q, start, Q_BLOCK, axis=1)
        s = jnp.einsum('bqhd,bkhd->bhqk', qb, k).astype(jnp.float32)
        q_pos = start + jnp.arange(Q_BLOCK)
        s = jnp.where(key_pos[None, :] <= q_pos[:, None], s, -jnp.inf)
        p = jax.nn.softmax(s, axis=-1).reshape(Bsz, DA_HEADS, 2, Q_BLOCK, S)
        a = p[:, :, 0] - lam * p[:, :, 1]
        return jnp.einsum('bhqk,bkhd->bqhd', a.astype(v.dtype), v)

    o = lax.map(block, jnp.arange(S // Q_BLOCK))
    o = jnp.moveaxis(o, 0, 1).reshape(Bsz, S, DA_HEADS, DA_V_DIM)
    o = rms_norm(o, sub_norm_w) * (1.0 - lambda_init)
    return o.reshape(Bsz, S, DA_WIDTH)


def rwkv7_time_mix(p, mu, w0, w2, a0, a2, g2, k_k, k_a, r_k, ln_w, ln_b):
    Bsz, S = p.shape[0], p.shape[1]
    f32 = jnp.float32
    p = p + (token_shift(p) - p) * mu
    r, k, v, w_lo, a_lo, g_lo = _split_last(p, RW_SIZES)
    w = -jax.nn.softplus(-(w0 + jnp.tanh(w_lo) @ w2)) - 0.5
    decay = jnp.exp(-jnp.exp(w.astype(f32)))
    a = jax.nn.sigmoid(a0 + a_lo @ a2)
    g = jax.nn.sigmoid(g_lo) @ g2
    heads = lambda t: t.reshape(Bsz, S, RW_HEADS, RW_HEAD).astype(f32)
    kk = heads(k * k_k)
    kk = kk * lax.rsqrt(jnp.maximum(jnp.sum(kk * kk, axis=-1, keepdims=True), 1e-24))
    k = k * (1.0 + (a - 1.0) * k_a)
    rh, kh, vh, ah, wh = heads(r), heads(k), heads(v), heads(a), heads(decay)
    bh = kk * ah

    def step(state, inp):
        r_t, w_t, k_t, v_t, kk_t, b_t = inp
        sa = jnp.einsum('bhvk,bhk->bhv', state, -kk_t)
        state = (state * w_t[:, :, None, :] + sa[..., None] * b_t[:, :, None, :]
                 + v_t[..., None] * k_t[:, :, None, :])
        return state, jnp.einsum('bhvk,bhk->bhv', state, r_t)

    xs = tuple(jnp.moveaxis(t, 1, 0) for t in (rh, wh, kh, vh, kk, bh))
    state0 = jnp.zeros((Bsz, RW_HEADS, RW_HEAD, RW_HEAD), f32)
    _, y = lax.scan(step, state0, xs)
    y = jnp.moveaxis(y, 0, 1)
    mean = jnp.mean(y, axis=-1, keepdims=True)
    var = jnp.mean(jnp.square(y - mean), axis=-1, keepdims=True)
    y = (y - mean) * lax.rsqrt(var + RW_GN_EPS)
    y = y * ln_w.astype(f32).reshape(RW_HEADS, RW_HEAD) + ln_b.astype(f32).reshape(RW_HEADS, RW_HEAD)
    y = y + jnp.sum(rh * kh * r_k.astype(f32), axis=-1, keepdims=True) * vh
    return (y.reshape(Bsz, S, RW_WIDTH) * g.astype(f32)).astype(p.dtype)


def causal_depthwise_conv(x, w, b):
    y = lax.conv_general_dilated(x, w[:, None, :].astype(x.dtype), window_strides=(1,),
                                 padding=[(SSM_CONV - 1, 0)],
                                 dimension_numbers=('NWC', 'WIO', 'NWC'),
                                 feature_group_count=x.shape[-1])
    return y + b


def ssd_chunked(xs, dt, A, Bm, Cm):
    Bsz, S, H, P = xs.shape
    G, N = Bm.shape[2], Bm.shape[3]
    E = H // G
    Lc = math.gcd(S, SSM_CHUNK)
    nc = S // Lc
    f32 = jnp.float32
    xdt = (xs.astype(f32) * dt[..., None]).reshape(Bsz, nc, Lc, G, E, P)
    dA = jnp.moveaxis((dt * A).reshape(Bsz, nc, Lc, G, E), 2, -1)
    cs = jnp.cumsum(dA, axis=-1)
    Bc = Bm.astype(f32).reshape(Bsz, nc, Lc, G, N)
    Cc = Cm.astype(f32).reshape(Bsz, nc, Lc, G, N)
    tri = jnp.tril(jnp.ones((Lc, Lc), dtype=bool))
    seg = jnp.exp(jnp.where(tri, cs[..., :, None] - cs[..., None, :], -jnp.inf))
    cb = jnp.einsum('bclgn,bcsgn->bcgls', Cc, Bc)
    y_diag = jnp.einsum('bcgls,bcgels,bcsgep->bclgep', cb, seg, xdt)
    decay_to_end = jnp.exp(cs[..., -1:] - cs)
    chunk_states = jnp.einsum('bclgn,bcgel,bclgep->bcgepn', Bc, decay_to_end, xdt)
    tot = jnp.cumsum(cs[..., -1], axis=1)
    tot_prev = jnp.concatenate([jnp.zeros_like(tot[:, :1]), tot[:, :-1]], axis=1)
    earlier = jnp.tril(jnp.ones((nc, nc), dtype=bool), k=-1)[None, :, :, None, None]
    w_chunk = jnp.exp(jnp.where(earlier, tot_prev[:, :, None] - tot[:, None, :], -jnp.inf))
    states_in = jnp.einsum('bzjge,bjgepn->bzgepn', w_chunk, chunk_states)
    y_off = jnp.einsum('bclgn,bcgepn,bcgel->bclgep', Cc, states_in, jnp.exp(cs))
    return (y_diag + y_off).reshape(Bsz, S, H, P)


def mamba2_mixer(z, xbc, dt_raw, conv_w, conv_b, dt_bias, a_log, d_skip, norm_w):
    Bsz, S = z.shape[0], z.shape[1]
    f32 = jnp.float32
    xbc = jax.nn.silu(causal_depthwise_conv(xbc, conv_w, conv_b))
    xs, Bm, Cm = _split_last(xbc, (SSM_WIDTH, SSM_GROUPS * SSM_STATE, SSM_GROUPS * SSM_STATE))
    xs = xs.reshape(Bsz, S, SSM_HEADS, SSM_HEAD)
    Bm = Bm.reshape(Bsz, S, SSM_GROUPS, SSM_STATE)
    Cm = Cm.reshape(Bsz, S, SSM_GROUPS, SSM_STATE)
    dt = jax.nn.softplus((dt_raw + dt_bias).astype(f32))
    A = -jnp.exp(a_log.astype(f32))
    y = ssd_chunked(xs, dt, A, Bm, Cm) + d_skip.astype(f32)[:, None] * xs.astype(f32)
    y = y.reshape(Bsz, S, SSM_WIDTH) * jax.nn.silu(z.astype(f32))
    y = y.reshape(Bsz, S, SSM_GROUPS, SSM_WIDTH // SSM_GROUPS)
    y = y * lax.rsqrt(jnp.mean(y * y, axis=-1, keepdims=True) + SSM_NORM_EPS)
    y = y.reshape(Bsz, S, SSM_WIDTH) * norm_w.astype(f32)
    return y.astype(z.dtype)


def peer_ffn(x, w_query, sub_keys, down, up):
    Bsz, S, D = x.shape
    T = Bsz * S
    blk = math.gcd(T, PEER_TOKEN_BLOCK)
    half = PEER_QDIM // 2
    n_cand = PEER_TOPK * PEER_TOPK

    def block(xb):
        q = (xb @ w_query).reshape(blk, PEER_HEADS, 2, half)
        s = jnp.einsum('thpd,hpkd->thpk', q, sub_keys).astype(jnp.float32)
        s_top, i_top = lax.top_k(s, PEER_TOPK)
        cand = (s_top[:, :, 0, :, None] + s_top[:, :, 1, None, :]).reshape(blk, PEER_HEADS, n_cand)
        cand_id = (i_top[:, :, 0, :, None] * PEER_KEYS + i_top[:, :, 1, None, :]).reshape(blk, PEER_HEADS, n_cand)
        best, pos = lax.top_k(cand, PEER_TOPK)
        expert = jnp.take_along_axis(cand_id, pos, axis=-1).reshape(blk, PEER_HEADS * PEER_TOPK)
        g = jax.nn.softmax(best, axis=-1).reshape(blk, PEER_HEADS * PEER_TOPK)
        h = jnp.einsum('td,nd->tn', xb, down)
        h_sel = jnp.take_along_axis(h, expert, axis=-1)
        coef = (g * jax.nn.gelu(h_sel.astype(jnp.float32), approximate=False)).astype(xb.dtype)
        rows = jnp.arange(blk)[:, None]
        weights = jnp.zeros((blk, PEER_EXPERTS), xb.dtype).at[rows, expert].add(coef)
        return weights @ up

    y = lax.map(block, x.reshape(T // blk, blk, D))
    return y.reshape(Bsz, S, D)


def setup_inputs(seed: int = 0) -> dict:
    key = jax.random.key(seed)
    ks = jax.random.split(key, 32)
    L = DEPTH
    f32 = jnp.float32
    nrm = lambda k, shape, scale: jax.random.normal(k, shape, f32) * scale
    gain = lambda k, shape: 1.0 + 0.02 * jax.random.normal(k, shape, f32)
    row_scale = jnp.concatenate([jnp.full((DA_WIDTH,), DA_WIDTH ** -0.5, f32),
                                 jnp.full((RW_WIDTH,), RW_WIDTH ** -0.5, f32),
                                 jnp.full((SSM_WIDTH,), SSM_WIDTH ** -0.5, f32)])
    dt0 = jnp.exp(jax.random.uniform(ks[20], (L, SSM_HEADS), f32, math.log(1e-3), math.log(1e-1)))
    return {
        'x': nrm(ks[0], (BATCH, SEQ, D_MODEL), 1.0),
        'norm_mix_w': gain(ks[1], (L, D_MODEL)),
        'w_in': nrm(ks[2], (L, D_MODEL, IN_COLS), D_MODEL ** -0.5),
        'da_q_norm_w': gain(ks[3], (L, HEAD_DIM)),
        'da_k_norm_w': gain(ks[4], (L, HEAD_DIM)),
        'da_lambda': nrm(ks[5], (L, 4, HEAD_DIM), 0.1),
        'da_sub_norm_w': gain(ks[6], (L, DA_V_DIM)),
        'rw_mu': jax.random.uniform(ks[7], (L, RW_COLS), f32),
        'rw_w0': jax.random.uniform(ks[8], (L, RW_WIDTH), f32, -3.0, 1.0),
        'rw_w2': nrm(ks[9], (L, RW_DECAY_LORA, RW_WIDTH), RW_DECAY_LORA ** -0.5),
        'rw_a0': nrm(ks[10], (L, RW_WIDTH), 0.1),
        'rw_a2': nrm(ks[11], (L, RW_AAA_LORA, RW_WIDTH), RW_AAA_LORA ** -0.5),
        'rw_g2': nrm(ks[12], (L, RW_GATE_LORA, RW_WIDTH), RW_GATE_LORA ** -0.5),
        'rw_k_k': 0.85 + nrm(ks[13], (L, RW_WIDTH), 0.02),
        'rw_k_a': 1.0 + nrm(ks[14], (L, RW_WIDTH), 0.02),
        'rw_r_k': nrm(ks[15], (L, RW_HEADS, RW_HEAD), 0.1),
        'rw_ln_w': gain(ks[16], (L, RW_WIDTH)),
        'rw_ln_b': nrm(ks[17], (L, RW_WIDTH), 0.02),
        'ssm_conv_w': nrm(ks[18], (L, SSM_CONV, SSM_XBC), SSM_CONV ** -0.5),
        'ssm_conv_b': nrm(ks[19], (L, SSM_XBC), 0.02),
        'ssm_dt_bias': dt0 + jnp.log(-jnp.expm1(-dt0)),
        'ssm_a_log': jnp.log(jax.random.uniform(ks[21], (L, SSM_HEADS), f32, 1.0, 16.0)),
        'ssm_d': 1.0 + nrm(ks[22], (L, SSM_HEADS), 0.1),
        'ssm_norm_w': gain(ks[23], (L, SSM_WIDTH)),
        'w_branch': nrm(ks[24], (L, MIX_WIDTH, D_MODEL), 1.0) * row_scale[None, :, None],
        'w_out': nrm(ks[25], (L, D_MODEL, D_MODEL), D_MODEL ** -0.5),
        'norm_ffn_w': gain(ks[26], (L, D_MODEL)),
        'peer_w_query': nrm(ks[27], (L, D_MODEL, PEER_HEADS * PEER_QDIM), D_MODEL ** -0.5),
        'peer_sub_keys': nrm(ks[28], (L, PEER_HEADS, 2, PEER_KEYS, PEER_QDIM // 2), (PEER_QDIM // 2) ** -0.5),
        'peer_down': nrm(ks[29], (L, PEER_EXPERTS, D_MODEL), D_MODEL ** -0.5),
        'peer_up': nrm(ks[30], (L, PEER_EXPERTS, D_MODEL), (PEER_HEADS * PEER_TOPK) ** -0.5),
    }


def reference(x, norm_mix_w, w_in, da_q_norm_w, da_k_norm_w, da_lambda, da_sub_norm_w,
              rw_mu, rw_w0, rw_w2, rw_a0, rw_a2, rw_g2, rw_k_k, rw_k_a, rw_r_k, rw_ln_w, rw_ln_b,
              ssm_conv_w, ssm_conv_b, ssm_dt_bias, ssm_a_log, ssm_d, ssm_norm_w,
              w_branch, w_out, norm_ffn_w, peer_w_query, peer_sub_keys, peer_down, peer_up):
    Bsz, S, D = x.shape
    pos = jnp.arange(S, dtype=jnp.float32)
    inv_freq = ROPE_THETA ** (-jnp.arange(0, HEAD_DIM, 2, dtype=jnp.float32) / HEAD_DIM)
    ang = pos[:, None] * inv_freq[None, :]
    cos, sin = jnp.cos(ang), jnp.sin(ang)
    for l in range(DEPTH):
        lambda_init = 0.8 - 0.6 * math.exp(-0.3 * l)
        xn = rms_norm(x, norm_mix_w[l])
        proj = xn @ w_in[l]
        gate_in, da_q, da_k, da_v, rw_p, ssm_z, ssm_xbc, ssm_dt = _split_last(proj, IN_SIZES)
        gates = jax.nn.sigmoid(gate_in.astype(jnp.float32)).astype(x.dtype).reshape(Bsz, S, N_BRANCH, D)
        o_da = diff_attention(da_q.reshape(Bsz, S, 2 * DA_HEADS, HEAD_DIM),
                              da_k.reshape(Bsz, S, 2 * DA_HEADS, HEAD_DIM),
                              da_v.reshape(Bsz, S, DA_HEADS, DA_V_DIM),
                              cos, sin, da_q_norm_w[l], da_k_norm_w[l], da_lambda[l],
                              da_sub_norm_w[l], lambda_init)
        o_rw = rwkv7_time_mix(rw_p, rw_mu[l], rw_w0[l], rw_w2[l], rw_a0[l], rw_a2[l], rw_g2[l],
                              rw_k_k[l], rw_k_a[l], rw_r_k[l], rw_ln_w[l], rw_ln_b[l])
        o_ssm = mamba2_mixer(ssm_z, ssm_xbc, ssm_dt, ssm_conv_w[l], ssm_conv_b[l], ssm_dt_bias[l],
                             ssm_a_log[l], ssm_d[l], ssm_norm_w[l])
        wb = w_branch[l]
        merged = (gates[:, :, 0] * (o_da @ wb[:DA_WIDTH])
                  + gates[:, :, 1] * (o_rw @ wb[DA_WIDTH:DA_WIDTH + RW_WIDTH])
                  + gates[:, :, 2] * (o_ssm @ wb[DA_WIDTH + RW_WIDTH:]))
        x = x + merged @ w_out[l]
        x = x + peer_ffn(rms_norm(x, norm_ffn_w[l]), peer_w_query[l], peer_sub_keys[l],
                         peer_down[l], peer_up[l])
    return x
```

```python
import functools
import math

import jax
import jax.numpy as jnp
from jax import lax
from jax.experimental import pallas as pl
from jax.experimental.pallas import tpu as pltpu

F32 = jnp.float32
BF16 = jnp.bfloat16
HIGHEST = lax.Precision.HIGHEST

LANES = 128
VMEM_LIMIT_BYTES = 56 * 1024 * 1024

HEAD_DIM = 64
RMS_EPS = 1e-6
ROPE_THETA = 10000.0
RW_LORA = (64, 64, 128)
RW_GN_EPS = 64e-5
RW_CHUNK = 64
SSM_GROUPS = 4
SSM_STATE = 128
SSM_CONV = 4
SSM_CHUNK = 256
SSM_NORM_EPS = 1e-5
PEER_HEADS = 8
PEER_KEYS = 128
PEER_TOPK = 16


def _cparams(*sem):
    return pltpu.CompilerParams(dimension_semantics=sem, vmem_limit_bytes=VMEM_LIMIT_BYTES)


def _tile(n, pref):
    t = min(n, pref)
    assert n % t == 0, (n, pref)
    return t


def _nt_dot(a, b, **kw):
    return lax.dot_general(a, b, (((1,), (1,)), ((), ())), preferred_element_type=F32, **kw)


def _tn_dot(a, b, **kw):
    return lax.dot_general(a, b, (((0,), (0,)), ((), ())), preferred_element_type=F32, **kw)


def _rmsnorm_kernel(x_ref, w_ref, o_ref):
    x = x_ref[...]
    ms = jnp.mean(x * x, axis=-1, keepdims=True)
    o_ref[...] = (x * lax.rsqrt(ms + RMS_EPS) * w_ref[...]).astype(o_ref.dtype)


def rmsnorm_bf16(x, w):
    T, D = x.shape
    tm = _tile(T, 256)
    return pl.pallas_call(
        _rmsnorm_kernel,
        grid=(T // tm,),
        in_specs=[pl.BlockSpec((tm, D), lambda i: (i, 0)), pl.BlockSpec((1, D), lambda i: (0, 0))],
        out_specs=pl.BlockSpec((tm, D), lambda i: (i, 0)),
        out_shape=jax.ShapeDtypeStruct((T, D), BF16),
        compiler_params=_cparams("parallel"),
    )(x, w.reshape(1, D))


def _mm_kernel(a_ref, b_ref, o_ref):
    o_ref[...] = jnp.dot(a_ref[...], b_ref[...], preferred_element_type=F32).astype(o_ref.dtype)


def _mm_res_kernel(a_ref, b_ref, r_ref, o_ref):
    o_ref[...] = r_ref[...] + jnp.dot(a_ref[...], b_ref[...], preferred_element_type=F32)


def matmul(a, b, residual=None, out_dtype=F32, tm_pref=1024, tn_pref=512):
    M, K = a.shape
    N = b.shape[1]
    tm, tn = _tile(M, tm_pref), _tile(N, tn_pref)
    in_specs = [pl.BlockSpec((tm, K), lambda i, j: (i, 0)), pl.BlockSpec((K, tn), lambda i, j: (0, j))]
    args = [a, b]
    kern = _mm_kernel
    if residual is not None:
        in_specs.append(pl.BlockSpec((tm, tn), lambda i, j: (i, j)))
        args.append(residual)
        kern = _mm_res_kernel
    return pl.pallas_call(
        kern,
        grid=(M // tm, N // tn),
        in_specs=in_specs,
        out_specs=pl.BlockSpec((tm, tn), lambda i, j: (i, j)),
        out_shape=jax.ShapeDtypeStruct((M, N), out_dtype),
        compiler_params=_cparams("parallel", "arbitrary"),
    )(*args)


def _merge_kernel(g0_ref, g1_ref, g2_ref, a0_ref, a1_ref, a2_ref, w0_ref, w1_ref, w2_ref, o_ref):
    acc = jax.nn.sigmoid(g0_ref[...]) * jnp.dot(a0_ref[...], w0_ref[...], preferred_element_type=F32)
    acc += jax.nn.sigmoid(g1_ref[...]) * jnp.dot(a1_ref[...], w1_ref[...], preferred_element_type=F32)
    acc += jax.nn.sigmoid(g2_ref[...]) * jnp.dot(a2_ref[...], w2_ref[...], preferred_element_type=F32)
    o_ref[...] = acc.astype(o_ref.dtype)


def merge_branches(proj, o_da, o_rw, o_ssm, wb_da, wb_rw, wb_ssm, D):
    T = proj.shape[0]
    tm, tn = _tile(T, 512), _tile(D, 512)
    nb = D // tn

    def gate_spec(n):
        return pl.BlockSpec((tm, tn), lambda i, j: (i, n * nb + j))

    def act_spec(a):
        return pl.BlockSpec((tm, a.shape[1]), lambda i, j: (i, 0))

    def w_spec(w):
        return pl.BlockSpec((w.shape[0], tn), lambda i, j: (0, j))

    return pl.pallas_call(
        _merge_kernel,
        grid=(T // tm, nb),
        in_specs=[gate_spec(0), gate_spec(1), gate_spec(2), act_spec(o_da), act_spec(o_rw), act_spec(o_ssm),
                  w_spec(wb_da), w_spec(wb_rw), w_spec(wb_ssm)],
        out_specs=pl.BlockSpec((tm, tn), lambda i, j: (i, j)),
        out_shape=jax.ShapeDtypeStruct((T, D), BF16),
        compiler_params=_cparams("parallel", "arbitrary"),
    )(proj, proj, proj, o_da, o_rw, o_ssm, wb_da, wb_rw, wb_ssm)


def _diff_attn_kernel(lam_ref, q_ref, k_ref, v_ref, nw_ref, o_ref, m_ref, l_ref, acc_ref, *, tq, out_scale):
    qi = pl.program_id(2)
    q = q_ref[0]
    lane = lax.broadcasted_iota(jnp.int32, q.shape, 1)
    zero = jnp.zeros_like(q)
    q2 = jnp.concatenate([jnp.where(lane < HEAD_DIM, q, zero), jnp.where(lane >= HEAD_DIM, q, zero)], axis=0)
    m_ref[...] = jnp.full(m_ref.shape, -jnp.inf, F32)
    l_ref[...] = jnp.zeros(l_ref.shape, F32)
    acc_ref[...] = jnp.zeros(acc_ref.shape, F32)

    def step(j, masked):
        kt = k_ref[0, pl.ds(j * tq, tq), :]
        vt = v_ref[0, pl.ds(j * tq, tq), :]
        s = _nt_dot(q2, kt)
        if masked:
            row = lax.broadcasted_iota(jnp.int32, (2 * tq, tq), 0)
            row = jnp.where(row >= tq, row - tq, row)
            col = lax.broadcasted_iota(jnp.int32, (2 * tq, tq), 1)
            s = jnp.where(col <= row, s, -jnp.inf)
        m_old = m_ref[...]
        m_new = jnp.maximum(m_old, jnp.max(s, axis=-1, keepdims=True))
        alpha = jnp.exp(m_old - m_new)
        p = jnp.exp(s - m_new)
        l_ref[...] = alpha * l_ref[...] + jnp.sum(p, axis=-1, keepdims=True)
        acc_ref[...] = alpha * acc_ref[...] + jnp.dot(p.astype(vt.dtype), vt, preferred_element_type=F32)
        m_ref[...] = m_new

    def body(j, c):
        step(j, False)
        return c

    lax.fori_loop(0, qi, body, 0)
    step(qi, True)
    o = acc_ref[...] / l_ref[...]
    o = o[:tq] - lam_ref[0] * o[tq:]
    ms = jnp.mean(o * o, axis=-1, keepdims=True)
    o_ref[0] = (o * lax.rsqrt(ms + RMS_EPS) * nw_ref[...] * out_scale).astype(o_ref.dtype)


def diff_attention(lam, q, k, v, sub_norm_w, lambda_init):
    B, S, W = q.shape
    H = W // LANES
    tq = _tile(S, 256)
    kern = functools.partial(_diff_attn_kernel, tq=tq, out_scale=1.0 - lambda_init)
    return pl.pallas_call(
        kern,
        grid=(B, H, S // tq),
        in_specs=[pl.BlockSpec(memory_space=pltpu.SMEM),
                  pl.BlockSpec((1, tq, LANES), lambda b, h, i: (b, i, h)),
                  pl.BlockSpec((1, S, LANES), lambda b, h, i: (b, 0, h)),
                  pl.BlockSpec((1, S, LANES), lambda b, h, i: (b, 0, h)),
                  pl.BlockSpec((1, LANES), lambda b, h, i: (0, 0))],
        out_specs=pl.BlockSpec((1, tq, LANES), lambda b, h, i: (b, i, h)),
        out_shape=jax.ShapeDtypeStruct((B, S, W), BF16),
        scratch_shapes=[pltpu.VMEM((2 * tq, 1), F32), pltpu.VMEM((2 * tq, 1), F32),
                        pltpu.VMEM((2 * tq, LANES), F32)],
        compiler_params=_cparams("parallel", "parallel", "arbitrary"),
    )(lam, q, k, v, sub_norm_w.reshape(1, LANES))


def _rotary(x, cos, sin):
    x1, x2 = jnp.split(x, 2, axis=-1)
    c, s = cos[None, :, None, :], sin[None, :, None, :]
    return jnp.concatenate([x1 * c - x2 * s, x2 * c + x1 * s], axis=-1)


def _head_rms(x, w):
    return x * lax.rsqrt(jnp.mean(x * x, axis=-1, keepdims=True) + RMS_EPS) * w


def _rwkv_chunk_kernel(r_ref, lw_ref, k_ref, v_ref, a_ref, b_ref, y_ref, state_ref, *, hb, C):
    @pl.when(pl.program_id(2) == 0)
    def _():
        state_ref[...] = jnp.zeros(state_ref.shape, F32)

    ti = lax.broadcasted_iota(jnp.int32, (C, C), 0)
    si = lax.broadcasted_iota(jnp.int32, (C, C), 1)
    incl = si <= ti
    strict = si < ti
    tri = incl.astype(F32)
    dot = functools.partial(jnp.dot, preferred_element_type=F32, precision=HIGHEST)
    for h in range(hb):
        r, lw, k, v, a, b = (ref[0, h] for ref in (r_ref, lw_ref, k_ref, v_ref, a_ref, b_ref))
        cum = dot(tri, lw)
        inv = jnp.exp(-cum)
        ar = jnp.concatenate([a * jnp.exp(cum - lw), r * jnp.exp(cum)], axis=0)
        bk = jnp.concatenate([b * inv, k * inv], axis=0)
        g = _nt_dot(ar, bk, precision=HIGHEST)
        l_ab = jnp.where(strict, g[:C, :C], 0.0)
        l_ak = jnp.where(strict, g[:C, C:], 0.0)
        m_rb = jnp.where(incl, g[C:, :C], 0.0)
        m_rk = jnp.where(incl, g[C:, C:], 0.0)
        s0 = state_ref[h]
        x = _nt_dot(ar, s0, precision=HIGHEST)
        u = x[:C] + dot(l_ak, v)
        lp = l_ab
        n_double = max(1, (C - 1).bit_length())
        for it in range(n_double):
            u = u + dot(lp, u)
            if it + 1 < n_double:
                lp = dot(lp, lp)
        y_ref[0, h] = x[C:] + dot(m_rb, u) + dot(m_rk, v)
        to_end = jnp.exp(cum[C - 1:C, :] - cum)
        uv = jnp.concatenate([u, v], axis=0)
        bk_end = jnp.concatenate([b * to_end, k * to_end], axis=0)
        state_ref[h] = s0 * jnp.exp(cum[C - 1:C, :]) + _tn_dot(uv, bk_end, precision=HIGHEST)


def rwkv_scan(r, lw, k, v, a, b):
    B, H, S, N = r.shape
    C = _tile(S, RW_CHUNK)
    hb = _tile(H, 8)
    spec = pl.BlockSpec((1, hb, C, N), lambda bi, hi, ci: (bi, hi, ci, 0))
    return pl.pallas_call(
        functools.partial(_rwkv_chunk_kernel, hb=hb, C=C),
        grid=(B, H // hb, S // C),
        in_specs=[spec] * 6,
        out_specs=spec,
        out_shape=jax.ShapeDtypeStruct((B, H, S, N), F32),
        scratch_shapes=[pltpu.VMEM((hb, N, N), F32)],
        compiler_params=_cparams("parallel", "parallel", "arbitrary"),
    )(r, lw, k, v, a, b)


def _ssd_kernel(xs_ref, bm_ref, cm_ref, z_ref, dt_ref, dtt_ref, a_ref, acol_ref, d_ref, nw_ref, ex_ref, o_ref,
                state_ref, *, L, E, P):
    @pl.when(pl.program_id(2) == 0)
    def _():
        state_ref[...] = jnp.zeros(state_ref.shape, F32)

    li = lax.broadcasted_iota(jnp.int32, (L, L), 0)
    si = lax.broadcasted_iota(jnp.int32, (L, L), 1)
    lower = si <= li
    hdot = functools.partial(jnp.dot, preferred_element_type=F32, precision=HIGHEST)
    ex = ex_ref[...]
    a_row = a_ref[0]
    dt = dt_ref[0, 0]
    cs = hdot(lower.astype(F32), dt * a_row)
    cs_row = hdot(dtt_ref[0, 0] * acol_ref[0], (li <= si).astype(F32))
    cs_last = cs[L - 1:L, :]
    xs = xs_ref[0]
    xdt = xs * hdot(dt, ex)
    bm = bm_ref[0].astype(BF16)
    cm = cm_ref[0].astype(BF16)
    cb = _nt_dot(cm, bm)
    state_in = state_ref[...]
    y = hdot(jnp.exp(cs), ex) * jnp.dot(cm, state_in.astype(BF16), preferred_element_type=F32)
    y += d_ref[0] * xs
    lane = lax.broadcasted_iota(jnp.int32, (L, 2 * P), 1)
    ydiag = []
    for pr in range(E // 2):
        xpair = xdt[:, pr * 2 * P:(pr + 1) * 2 * P]
        acc = jnp.zeros((L, 2 * P), F32)
        for sub in range(2):
            e = 2 * pr + sub
            seg = jnp.where(lower, jnp.exp(jnp.minimum(cs[:, e:e + 1] - cs_row[e:e + 1, :], 0.0)), 0.0)
            xm = jnp.where((lane >= sub * P) & (lane < (sub + 1) * P), xpair, 0.0)
            acc += jnp.dot((cb * seg).astype(BF16), xm.astype(BF16), preferred_element_type=F32)
        ydiag.append(acc)
    y += jnp.concatenate(ydiag, axis=1)
    xend = (xdt * hdot(jnp.exp(cs_last - cs), ex)).astype(BF16)
    state_ref[...] = hdot(jnp.exp(cs_last), ex) * state_in + _tn_dot(bm, xend)
    zg = z_ref[...]
    y = y * (zg * jax.nn.sigmoid(zg))
    y = y * lax.rsqrt(jnp.mean(y * y, axis=-1, keepdims=True) + SSM_NORM_EPS)
    o_ref[...] = (y * nw_ref[...]).astype(o_ref.dtype)


def ssd_scan(xbc, z, dt, a_neg, d_skip, norm_w, B, S, H):
    G, N, P = SSM_GROUPS, SSM_STATE, HEAD_DIM
    E = H // G
    EP = E * P
    W = H * P
    L = math.gcd(S, SSM_CHUNK)
    nc = S // L
    dt_g = dt.reshape(B, S, G, E).transpose(0, 2, 1, 3)
    dt_gt = dt_g.transpose(0, 1, 3, 2)
    ex = (jnp.arange(EP)[None, :] // P == jnp.arange(E)[:, None]).astype(F32)
    d_full = jnp.repeat(d_skip.reshape(G, 1, E), P, axis=2)
    kern = functools.partial(_ssd_kernel, L=L, E=E, P=P)
    return pl.pallas_call(
        kern,
        grid=(B, G, nc),
        in_specs=[pl.BlockSpec((1, L, EP), lambda b, g, c: (b, c, g)),
                  pl.BlockSpec((1, L, N), lambda b, g, c: (b, c, W // N + g)),
                  pl.BlockSpec((1, L, N), lambda b, g, c: (b, c, W // N + G + g)),
                  pl.BlockSpec((L, EP), lambda b, g, c: (b * nc + c, g)),
                  pl.BlockSpec((1, 1, L, E), lambda b, g, c: (b, g, c, 0)),
                  pl.BlockSpec((1, 1, E, L), lambda b, g, c: (b, g, 0, c)),
                  pl.BlockSpec((1, 1, E), lambda b, g, c: (g, 0, 0)),
                  pl.BlockSpec((1, E, 1), lambda b, g, c: (g, 0, 0)),
                  pl.BlockSpec((1, 1, EP), lambda b, g, c: (g, 0, 0)),
                  pl.BlockSpec((1, EP), lambda b, g, c: (0, g)),
                  pl.BlockSpec((E, EP), lambda b, g, c: (0, 0))],
        out_specs=pl.BlockSpec((L, EP), lambda b, g, c: (b * nc + c, g)),
        out_shape=jax.ShapeDtypeStruct((B * S, W), BF16),
        scratch_shapes=[pltpu.VMEM((N, EP), F32)],
        compiler_params=_cparams("parallel", "parallel", "arbitrary"),
    )(xbc, xbc, xbc, z, dt_g, dt_gt, a_neg.reshape(G, 1, E), a_neg.reshape(G, E, 1), d_full,
      norm_w.reshape(1, W), ex)


def _peer_kernel(xn_ref, down_ref, up_ref, s1_ref, s0c_ref, tau_ref, c_ref, o_ref, *, tm, tn):
    j = pl.program_id(1)

    @pl.when(j == 0)
    def _():
        o_ref[...] = jnp.zeros(o_ref.shape, F32)

    hid = _nt_dot(xn_ref[...], down_ref[...])
    na = tn // PEER_KEYS
    s0c = s0c_ref[0]
    tau = tau_ref[...]
    cc = c_ref[...]
    gates = []
    for ai in range(na):
        g = jnp.zeros((tm, PEER_KEYS), F32)
        for hd in range(PEER_HEADS):
            z = s0c[:, hd * na + ai:hd * na + ai + 1] + s1_ref[:, hd * PEER_KEYS:(hd + 1) * PEER_KEYS]
            g += jnp.where(z >= tau[:, hd:hd + 1], jnp.exp(z - cc[:, hd:hd + 1]), 0.0)
        gates.append(g)
    gate = gates[0] if na == 1 else jnp.concatenate(gates, axis=1)
    act = 0.5 * hid * (1.0 + lax.erf(hid * (2.0 ** -0.5)))
    o_ref[...] += jnp.dot((gate * act).astype(BF16), up_ref[...], preferred_element_type=F32)


def peer_dense(xn, down, up, s, tau, c):
    T, D = xn.shape
    NE = down.shape[0]
    tm, tn = _tile(T, 512), _tile(NE, 256)
    na = tn // PEER_KEYS
    nj = NE // tn
    s4 = s.reshape(T, PEER_HEADS, 2, PEER_KEYS)
    s1 = s4[:, :, 1, :].reshape(T, PEER_HEADS * PEER_KEYS)
    s0c = s4[:, :, 0, :].reshape(T, PEER_HEADS, nj, na).transpose(2, 0, 1, 3).reshape(nj, T, PEER_HEADS * na)
    return pl.pallas_call(
        functools.partial(_peer_kernel, tm=tm, tn=tn),
        grid=(T // tm, nj),
        in_specs=[pl.BlockSpec((tm, D), lambda i, j: (i, 0)),
                  pl.BlockSpec((tn, D), lambda i, j: (j, 0)),
                  pl.BlockSpec((tn, D), lambda i, j: (j, 0)),
                  pl.BlockSpec((tm, PEER_HEADS * PEER_KEYS), lambda i, j: (i, 0)),
                  pl.BlockSpec((1, tm, PEER_HEADS * na), lambda i, j: (j, i, 0)),
                  pl.BlockSpec((tm, PEER_HEADS), lambda i, j: (i, 0)),
                  pl.BlockSpec((tm, PEER_HEADS), lambda i, j: (i, 0))],
        out_specs=pl.BlockSpec((tm, D), lambda i, j: (i, 0)),
        out_shape=jax.ShapeDtypeStruct((T, D), F32),
        compiler_params=_cparams("parallel", "arbitrary"),
    )(xn, down, up, s1, s0c, tau, c)


def _pad_cols(w, mult):
    pad = (-w.shape[1]) % mult
    return jnp.pad(w, ((0, 0), (0, pad))) if pad else w


def kernel(x, norm_mix_w, w_in, da_q_norm_w, da_k_norm_w, da_lambda, da_sub_norm_w, rw_mu, rw_w0, rw_w2, rw_a0, rw_a2, rw_g2, rw_k_k, rw_k_a, rw_r_k, rw_ln_w, rw_ln_b, ssm_conv_w, ssm_conv_b, ssm_dt_bias, ssm_a_log, ssm_d, ssm_norm_w, w_branch, w_out, norm_ffn_w, peer_w_query, peer_sub_keys, peer_down, peer_up):
    B, S, D = x.shape
    T = B * S
    depth = w_in.shape[0]
    da_heads = D // 512
    da_cols = 2 * da_heads * HEAD_DIM
    da_width = da_heads * 2 * HEAD_DIM
    rw_heads = D // 256
    rw_width = rw_heads * HEAD_DIM
    rw_sizes = (rw_width, rw_width, rw_width) + RW_LORA
    rw_cols = sum(rw_sizes)
    ssm_heads = D // 128
    ssm_width = ssm_heads * HEAD_DIM
    ssm_xbc = ssm_width + 2 * SSM_GROUPS * SSM_STATE
    sizes = (3 * D, da_cols, da_cols, da_width, rw_cols, ssm_width, ssm_xbc, ssm_heads)
    offs = [0]
    for sz in sizes:
        offs.append(offs[-1] + sz)
    assert offs[-1] == w_in.shape[2]
    c_q, c_k, c_v, c_rw, c_z, c_xbc, c_dt = offs[1:8]

    pos = jnp.arange(S, dtype=F32)
    inv_freq = ROPE_THETA ** (-jnp.arange(0, HEAD_DIM, 2, dtype=F32) / HEAD_DIM)
    ang = pos[:, None] * inv_freq[None, :]
    cos, sin = jnp.cos(ang), jnp.sin(ang)

    xt = x.reshape(T, D)
    for l in range(depth):
        lambda_init = 0.8 - 0.6 * math.exp(-0.3 * l)
        xn = rmsnorm_bf16(xt, norm_mix_w[l])
        proj = matmul(xn, _pad_cols(w_in[l].astype(BF16), 512))

        q = proj[:, c_q:c_q + da_cols].reshape(B, S, 2 * da_heads, HEAD_DIM)
        k = proj[:, c_k:c_k + da_cols].reshape(B, S, 2 * da_heads, HEAD_DIM)
        q = _rotary(_head_rms(q, da_q_norm_w[l]), cos, sin) * (HEAD_DIM ** -0.5)
        k = _rotary(_head_rms(k, da_k_norm_w[l]), cos, sin)
        v = proj[:, c_v:c_v + da_width].reshape(B, S, da_width)
        lv = da_lambda[l]
        lam = jnp.exp(jnp.sum(lv[0] * lv[1])) - jnp.exp(jnp.sum(lv[2] * lv[3])) + lambda_init
        o_da = diff_attention(lam.reshape(1), q.reshape(B, S, da_cols).astype(BF16),
                              k.reshape(B, S, da_cols).astype(BF16), v.astype(BF16), da_sub_norm_w[l], lambda_init)
        o_da = o_da.reshape(T, da_width)

        p = proj[:, c_rw:c_rw + rw_cols].reshape(B, S, rw_cols)
        p_prev = jnp.pad(p, ((0, 0), (1, 0), (0, 0)))[:, :-1]
        p = p + (p_prev - p) * rw_mu[l]
        r_, k_, v_, w_lo, a_lo, g_lo = jnp.split(p, [sum(rw_sizes[:n]) for n in range(1, len(rw_sizes))], axis=-1)
        w_ = -jax.nn.softplus(-(rw_w0[l] + jnp.tanh(w_lo) @ rw_w2[l])) - 0.5
        logw = -jnp.exp(w_)
        a_ = jax.nn.sigmoid(rw_a0[l] + a_lo @ rw_a2[l])
        g_ = jax.nn.sigmoid(g_lo) @ rw_g2[l]
        heads = lambda t: t.reshape(B, S, rw_heads, HEAD_DIM).transpose(0, 2, 1, 3)
        kk = heads(k_ * rw_k_k[l])
        kk = kk * lax.rsqrt(jnp.maximum(jnp.sum(kk * kk, axis=-1, keepdims=True), 1e-24))
        k_ = k_ * (1.0 + (a_ - 1.0) * rw_k_a[l])
        rh, kh, vh, ah = heads(r_), heads(k_), heads(v_), heads(a_)
        y = rwkv_scan(rh, heads(logw), kh, vh, -kk, kk * ah)
        mean = jnp.mean(y, axis=-1, keepdims=True)
        var = jnp.mean(jnp.square(y - mean), axis=-1, keepdims=True)
        y = (y - mean) * lax.rsqrt(var + RW_GN_EPS)
        y = y * rw_ln_w[l].reshape(1, rw_heads, 1, HEAD_DIM) + rw_ln_b[l].reshape(1, rw_heads, 1, HEAD_DIM)
        y = y + jnp.sum(rh * kh * rw_r_k[l][None, :, None, :], axis=-1, keepdims=True) * vh
        o_rw = (y.transpose(0, 2, 1, 3).reshape(B, S, rw_width) * g_).astype(BF16).reshape(T, rw_width)

        xbc = proj[:, c_xbc:c_xbc + ssm_xbc].reshape(B, S, ssm_xbc)
        xpad = jnp.pad(xbc, ((0, 0), (SSM_CONV - 1, 0), (0, 0)))
        conv = ssm_conv_b[l] + sum(ssm_conv_w[l][jj] * xpad[:, jj:jj + S] for jj in range(SSM_CONV))
        xbc = conv * jax.nn.sigmoid(conv)
        dt = jax.nn.softplus(proj[:, c_dt:c_dt + ssm_heads].reshape(B, S, ssm_heads) + ssm_dt_bias[l])
        o_ssm = ssd_scan(xbc, proj[:, c_z:c_z + ssm_width], dt, -jnp.exp(ssm_a_log[l]), ssm_d[l], ssm_norm_w[l],
                         B, S, ssm_heads)

        wb = w_branch[l].astype(BF16)
        merged = merge_branches(proj, o_da, o_rw, o_ssm, wb[:da_width], wb[da_width:da_width + rw_width],
                                wb[da_width + rw_width:], D)
        xt = matmul(merged, w_out[l].astype(BF16), residual=xt)

        xn2 = rmsnorm_bf16(xt, norm_ffn_w[l])
        qp = matmul(xn2, peer_w_query[l].astype(BF16))
        half = qp.shape[1] // (2 * PEER_HEADS)
        s = jnp.einsum('thpd,hpkd->thpk', qp.reshape(T, PEER_HEADS, 2, half), peer_sub_keys[l])
        s_top, _ = lax.top_k(s, PEER_TOPK)
        cand = (s_top[:, :, 0, :, None] + s_top[:, :, 1, None, :]).reshape(T, PEER_HEADS, PEER_TOPK * PEER_TOPK)
        best, _ = lax.top_k(cand, PEER_TOPK)
        tau = best[..., PEER_TOPK - 1]
        c = best[..., 0] + jnp.log(jnp.sum(jnp.exp(best - best[..., :1]), axis=-1))
        yp = peer_dense(xn2, peer_down[l].astype(BF16), peer_up[l].astype(BF16),
                        s.reshape(T, PEER_HEADS * 2 * PEER_KEYS), tau, c)
        xt = xt + yp
    return xt.reshape(B, S, D)
```

```python
import functools
import math

import jax
import jax.numpy as jnp
from jax import lax
from jax.experimental import pallas as pl
from jax.experimental.pallas import tpu as pltpu

F32 = jnp.float32
BF16 = jnp.bfloat16
HIGHEST = lax.Precision.HIGHEST

LANES = 128
VMEM_LIMIT_BYTES = 56 * 1024 * 1024

HEAD_DIM = 64
RMS_EPS = 1e-6
ROPE_THETA = 10000.0
RW_LORA = (64, 64, 128)
RW_GN_EPS = 64e-5
RW_CHUNK = 64
SSM_GROUPS = 4
SSM_STATE = 128
SSM_CONV = 4
SSM_CHUNK = 256
SSM_NORM_EPS = 1e-5
PEER_HEADS = 8
PEER_KEYS = 128
PEER_TOPK = 16


def _cparams(*sem):
    return pltpu.CompilerParams(dimension_semantics=sem, vmem_limit_bytes=VMEM_LIMIT_BYTES)


def _tile(n, pref):
    t = min(n, pref)
    assert n % t == 0, (n, pref)
    return t


def _nt_dot(a, b, **kw):
    return lax.dot_general(a, b, (((1,), (1,)), ((), ())), preferred_element_type=F32, **kw)


def _tn_dot(a, b, **kw):
    return lax.dot_general(a, b, (((0,), (0,)), ((), ())), preferred_element_type=F32, **kw)


def _rmsnorm_kernel(x_ref, w_ref, o_ref):
    x = x_ref[...]
    ms = jnp.mean(x * x, axis=-1, keepdims=True)
    o_ref[...] = (x * lax.rsqrt(ms + RMS_EPS) * w_ref[...]).astype(o_ref.dtype)


def rmsnorm_bf16(x, w):
    T, D = x.shape
    tm = _tile(T, 256)
    return pl.pallas_call(
        _rmsnorm_kernel,
        grid=(T // tm,),
        in_specs=[pl.BlockSpec((tm, D), lambda i: (i, 0)), pl.BlockSpec((1, D), lambda i: (0, 0))],
        out_specs=pl.BlockSpec((tm, D), lambda i: (i, 0)),
        out_shape=jax.ShapeDtypeStruct((T, D), BF16),
        compiler_params=_cparams("parallel"),
    )(x, w.reshape(1, D))


def _mm_kernel(a_ref, b_ref, o_ref):
    o_ref[...] = jnp.dot(a_ref[...], b_ref[...], preferred_element_type=F32).astype(o_ref.dtype)


def _mm_res_kernel(a_ref, b_ref, r_ref, o_ref):
    o_ref[...] = r_ref[...] + jnp.dot(a_ref[...], b_ref[...], preferred_element_type=F32)


def matmul(a, b, residual=None, out_dtype=F32, tm_pref=1024, tn_pref=512):
    M, K = a.shape
    N = b.shape[1]
    tm, tn = _tile(M, tm_pref), _tile(N, tn_pref)
    in_specs = [pl.BlockSpec((tm, K), lambda i, j: (i, 0)), pl.BlockSpec((K, tn), lambda i, j: (0, j))]
    args = [a, b]
    kern = _mm_kernel
    if residual is not None:
        in_specs.append(pl.BlockSpec((tm, tn), lambda i, j: (i, j)))
        args.append(residual)
        kern = _mm_res_kernel
    return pl.pallas_call(
        kern,
        grid=(M // tm, N // tn),
        in_specs=in_specs,
        out_specs=pl.BlockSpec((tm, tn), lambda i, j: (i, j)),
        out_shape=jax.ShapeDtypeStruct((M, N), out_dtype),
        compiler_params=_cparams("parallel", "arbitrary"),
    )(*args)


def _merge_kernel(g0_ref, g1_ref, g2_ref, a0_ref, a1_ref, a2_ref, w0_ref, w1_ref, w2_ref, o_ref):
    acc = jax.nn.sigmoid(g0_ref[...]) * jnp.dot(a0_ref[...], w0_ref[...], preferred_element_type=F32)
    acc += jax.nn.sigmoid(g1_ref[...]) * jnp.dot(a1_ref[...], w1_ref[...], preferred_element_type=F32)
    acc += jax.nn.sigmoid(g2_ref[...]) * jnp.dot(a2_ref[...], w2_ref[...], preferred_element_type=F32)
    o_ref[...] = acc.astype(o_ref.dtype)


def merge_branches(proj, o_da, o_rw, o_ssm, wb_da, wb_rw, wb_ssm, D):
    T = proj.shape[0]
    tm, tn = _tile(T, 512), _tile(D, 512)
    nb = D // tn

    def gate_spec(n):
        return pl.BlockSpec((tm, tn), lambda i, j: (i, n * nb + j))

    def act_spec(a):
        return pl.BlockSpec((tm, a.shape[1]), lambda i, j: (i, 0))

    def w_spec(w):
        return pl.BlockSpec((w.shape[0], tn), lambda i, j: (0, j))

    return pl.pallas_call(
        _merge_kernel,
        grid=(T // tm, nb),
        in_specs=[gate_spec(0), gate_spec(1), gate_spec(2), act_spec(o_da), act_spec(o_rw), act_spec(o_ssm),
                  w_spec(wb_da), w_spec(wb_rw), w_spec(wb_ssm)],
        out_specs=pl.BlockSpec((tm, tn), lambda i, j: (i, j)),
        out_shape=jax.ShapeDtypeStruct((T, D), BF16),
        compiler_params=_cparams("parallel", "arbitrary"),
    )(proj, proj, proj, o_da, o_rw, o_ssm, wb_da, wb_rw, wb_ssm)


def _diff_attn_kernel(lam_ref, qt_ref, k_ref, vt_ref, nw_ref, o_ref, m_ref, l_ref, acc_ref, *, tq, out_scale):
    qi = pl.program_id(2)
    qt = qt_ref[0, 0, 0]
    dim = lax.broadcasted_iota(jnp.int32, qt.shape, 0)
    zero = jnp.zeros_like(qt)
    qmaps = (jnp.where(dim < HEAD_DIM, qt, zero), jnp.where(dim >= HEAD_DIM, qt, zero))
    m_ref[...] = jnp.full(m_ref.shape, -jnp.inf, F32)
    l_ref[...] = jnp.zeros(l_ref.shape, F32)
    acc_ref[...] = jnp.zeros(acc_ref.shape, F32)

    def step(j, masked):
        kt = k_ref[0, pl.ds(j * tq, tq), :]
        vt = vt_ref[0, 0, j]
        scores = [jnp.dot(kt, qc, preferred_element_type=F32) for qc in qmaps]
        for c, s in enumerate(scores):
            if masked:
                key = lax.broadcasted_iota(jnp.int32, (tq, tq), 0)
                qry = lax.broadcasted_iota(jnp.int32, (tq, tq), 1)
                s = jnp.where(key <= qry, s, -jnp.inf)
            m_old = m_ref[c]
            m_new = jnp.maximum(m_old, jnp.max(s, axis=0, keepdims=True))
            alpha = jnp.exp(m_old - m_new)
            p = jnp.exp(s - m_new)
            l_ref[c] = alpha * l_ref[c] + jnp.sum(p, axis=0, keepdims=True)
            acc_ref[c] = alpha * acc_ref[c] + jnp.dot(vt, p.astype(vt.dtype), preferred_element_type=F32)
            m_ref[c] = m_new

    def body(j, c):
        step(j, False)
        return c

    lax.fori_loop(0, qi, body, 0)
    step(qi, True)
    o = acc_ref[0] / l_ref[0] - lam_ref[0] * (acc_ref[1] / l_ref[1])
    ms = jnp.mean(o * o, axis=0, keepdims=True)
    o = o * lax.rsqrt(ms + RMS_EPS) * nw_ref[...] * out_scale
    o_ref[0] = o.T.astype(o_ref.dtype)


def diff_attention(lam, q, k, v, sub_norm_w, lambda_init):
    B, S, W = q.shape
    H = W // LANES
    tq = _tile(S, 512)
    nq = S // tq
    to_t = lambda t: t.reshape(B, nq, tq, H, LANES).transpose(0, 3, 1, 4, 2)
    kern = functools.partial(_diff_attn_kernel, tq=tq, out_scale=1.0 - lambda_init)
    return pl.pallas_call(
        kern,
        grid=(B, H, nq),
        in_specs=[pl.BlockSpec(memory_space=pltpu.SMEM),
                  pl.BlockSpec((1, 1, 1, LANES, tq), lambda b, h, i: (b, h, i, 0, 0)),
                  pl.BlockSpec((1, S, LANES), lambda b, h, i: (b, 0, h)),
                  pl.BlockSpec((1, 1, nq, LANES, tq), lambda b, h, i: (b, h, 0, 0, 0)),
                  pl.BlockSpec((LANES, 1), lambda b, h, i: (0, 0))],
        out_specs=pl.BlockSpec((1, tq, LANES), lambda b, h, i: (b, i, h)),
        out_shape=jax.ShapeDtypeStruct((B, S, W), BF16),
        scratch_shapes=[pltpu.VMEM((2, 1, tq), F32), pltpu.VMEM((2, 1, tq), F32),
                        pltpu.VMEM((2, LANES, tq), F32)],
        compiler_params=_cparams("parallel", "parallel", "arbitrary"),
    )(lam, to_t(q), k, to_t(v), sub_norm_w.reshape(LANES, 1))


def _rotary(x, cos, sin):
    x1, x2 = jnp.split(x, 2, axis=-1)
    c, s = cos[None, :, None, :], sin[None, :, None, :]
    return jnp.concatenate([x1 * c - x2 * s, x2 * c + x1 * s], axis=-1)


def _head_rms(x, w):
    return x * lax.rsqrt(jnp.mean(x * x, axis=-1, keepdims=True) + RMS_EPS) * w


def _rwkv_chunk_kernel(r_ref, lw_ref, k_ref, v_ref, a_ref, b_ref, y_ref, state_ref, *, hb, C):
    @pl.when(pl.program_id(2) == 0)
    def _():
        state_ref[...] = jnp.zeros(state_ref.shape, F32)

    ti = lax.broadcasted_iota(jnp.int32, (C, C), 0)
    si = lax.broadcasted_iota(jnp.int32, (C, C), 1)
    incl = si <= ti
    strict = si < ti
    tri = incl.astype(F32)
    bf = lambda t: t.astype(BF16)
    dot = lambda x, y: jnp.dot(bf(x), bf(y), preferred_element_type=F32)
    hs = range(hb)
    r, lw, k, v, a, b = ([ref[0, h] for h in hs] for ref in (r_ref, lw_ref, k_ref, v_ref, a_ref, b_ref))
    cum = [jnp.dot(tri, lw[h], preferred_element_type=F32, precision=HIGHEST) for h in hs]
    ar = [bf(jnp.concatenate([a[h] * jnp.exp(cum[h] - lw[h]), r[h] * jnp.exp(cum[h])], axis=0)) for h in hs]
    inv = [jnp.exp(-cum[h]) for h in hs]
    bk = [bf(jnp.concatenate([b[h] * inv[h], k[h] * inv[h]], axis=0)) for h in hs]
    g = [_nt_dot(ar[h], bk[h]) for h in hs]
    s0 = [state_ref[h] for h in hs]
    x = [_nt_dot(ar[h], bf(s0[h])) for h in hs]
    u = [x[h][:C] + dot(jnp.where(strict, g[h][:C, C:], 0.0), v[h]) for h in hs]
    lp = [jnp.where(strict, g[h][:C, :C], 0.0) for h in hs]
    n_double = max(1, (C - 1).bit_length())
    for it in range(n_double):
        u = [u[h] + dot(lp[h], u[h]) for h in hs]
        if it + 1 < n_double:
            lp = [dot(lp[h], lp[h]) for h in hs]
    for h in hs:
        m_rb = jnp.where(incl, g[h][C:, :C], 0.0)
        m_rk = jnp.where(incl, g[h][C:, C:], 0.0)
        y_ref[0, h] = x[h][C:] + dot(m_rb, u[h]) + dot(m_rk, v[h])
    for h in hs:
        to_end = jnp.exp(cum[h][C - 1:C, :] - cum[h])
        uv = jnp.concatenate([u[h], v[h]], axis=0)
        bk_end = jnp.concatenate([b[h] * to_end, k[h] * to_end], axis=0)
        state_ref[h] = s0[h] * jnp.exp(cum[h][C - 1:C, :]) + _tn_dot(bf(uv), bf(bk_end))


def rwkv_scan(r, lw, k, v, a, b):
    B, H, S, N = r.shape
    C = _tile(S, RW_CHUNK)
    hb = _tile(H, 8)
    spec = pl.BlockSpec((1, hb, C, N), lambda bi, hi, ci: (bi, hi, ci, 0))
    return pl.pallas_call(
        functools.partial(_rwkv_chunk_kernel, hb=hb, C=C),
        grid=(B, H // hb, S // C),
        in_specs=[spec] * 6,
        out_specs=spec,
        out_shape=jax.ShapeDtypeStruct((B, H, S, N), F32),
        scratch_shapes=[pltpu.VMEM((hb, N, N), F32)],
        compiler_params=_cparams("parallel", "parallel", "arbitrary"),
    )(r, lw, k, v, a, b)


def _ssd_kernel(xs_ref, bm_ref, cm_ref, z_ref, dt_ref, dtt_ref, a_ref, acol_ref, d_ref, nw_ref, ex_ref, o_ref,
                state_ref, *, L, E, P):
    @pl.when(pl.program_id(2) == 0)
    def _():
        state_ref[...] = jnp.zeros(state_ref.shape, F32)

    li = lax.broadcasted_iota(jnp.int32, (L, L), 0)
    si = lax.broadcasted_iota(jnp.int32, (L, L), 1)
    lower = si <= li
    hdot = functools.partial(jnp.dot, preferred_element_type=F32, precision=HIGHEST)
    ex = ex_ref[...]
    a_row = a_ref[0]
    dt = dt_ref[0, 0]
    cs = hdot(lower.astype(F32), dt * a_row)
    cs_row = hdot(dtt_ref[0, 0] * acol_ref[0], (li <= si).astype(F32))
    cs_last = cs[L - 1:L, :]
    xs = xs_ref[0]
    xdt = xs * hdot(dt, ex)
    bm = bm_ref[0].astype(BF16)
    cm = cm_ref[0].astype(BF16)
    cb = _nt_dot(cm, bm)
    state_in = state_ref[...]
    y = hdot(jnp.exp(cs), ex) * jnp.dot(cm, state_in.astype(BF16), preferred_element_type=F32)
    y += d_ref[0] * xs
    lane = lax.broadcasted_iota(jnp.int32, (L, 2 * P), 1)
    ydiag = []
    for pr in range(E // 2):
        xpair = xdt[:, pr * 2 * P:(pr + 1) * 2 * P]
        acc = jnp.zeros((L, 2 * P), F32)
        for sub in range(2):
            e = 2 * pr + sub
            seg = jnp.where(lower, jnp.exp(jnp.minimum(cs[:, e:e + 1] - cs_row[e:e + 1, :], 0.0)), 0.0)
            xm = jnp.where((lane >= sub * P) & (lane < (sub + 1) * P), xpair, 0.0)
            acc += jnp.dot((cb * seg).astype(BF16), xm.astype(BF16), preferred_element_type=F32)
        ydiag.append(acc)
    y += jnp.concatenate(ydiag, axis=1)
    xend = (xdt * hdot(jnp.exp(cs_last - cs), ex)).astype(BF16)
    state_ref[...] = hdot(jnp.exp(cs_last), ex) * state_in + _tn_dot(bm, xend)
    zg = z_ref[...]
    y = y * (zg * jax.nn.sigmoid(zg))
    y = y * lax.rsqrt(jnp.mean(y * y, axis=-1, keepdims=True) + SSM_NORM_EPS)
    o_ref[...] = (y * nw_ref[...]).astype(o_ref.dtype)


def ssd_scan(xbc, z, dt, a_neg, d_skip, norm_w, B, S, H):
    G, N, P = SSM_GROUPS, SSM_STATE, HEAD_DIM
    E = H // G
    EP = E * P
    W = H * P
    L = math.gcd(S, SSM_CHUNK)
    nc = S // L
    dt_g = dt.reshape(B, S, G, E).transpose(0, 2, 1, 3)
    dt_gt = dt_g.transpose(0, 1, 3, 2)
    ex = (jnp.arange(EP)[None, :] // P == jnp.arange(E)[:, None]).astype(F32)
    d_full = jnp.repeat(d_skip.reshape(G, 1, E), P, axis=2)
    kern = functools.partial(_ssd_kernel, L=L, E=E, P=P)
    return pl.pallas_call(
        kern,
        grid=(B, G, nc),
        in_specs=[pl.BlockSpec((1, L, EP), lambda b, g, c: (b, c, g)),
                  pl.BlockSpec((1, L, N), lambda b, g, c: (b, c, W // N + g)),
                  pl.BlockSpec((1, L, N), lambda b, g, c: (b, c, W // N + G + g)),
                  pl.BlockSpec((L, EP), lambda b, g, c: (b * nc + c, g)),
                  pl.BlockSpec((1, 1, L, E), lambda b, g, c: (b, g, c, 0)),
                  pl.BlockSpec((1, 1, E, L), lambda b, g, c: (b, g, 0, c)),
                  pl.BlockSpec((1, 1, E), lambda b, g, c: (g, 0, 0)),
                  pl.BlockSpec((1, E, 1), lambda b, g, c: (g, 0, 0)),
                  pl.BlockSpec((1, 1, EP), lambda b, g, c: (g, 0, 0)),
                  pl.BlockSpec((1, EP), lambda b, g, c: (0, g)),
                  pl.BlockSpec((E, EP), lambda b, g, c: (0, 0))],
        out_specs=pl.BlockSpec((L, EP), lambda b, g, c: (b * nc + c, g)),
        out_shape=jax.ShapeDtypeStruct((B * S, W), BF16),
        scratch_shapes=[pltpu.VMEM((N, EP), F32)],
        compiler_params=_cparams("parallel", "parallel", "arbitrary"),
    )(xbc, xbc, xbc, z, dt_g, dt_gt, a_neg.reshape(G, 1, E), a_neg.reshape(G, E, 1), d_full,
      norm_w.reshape(1, W), ex)


def _extract_top(v, n):
    rows = []
    for _ in range(n):
        m = jnp.max(v, axis=0, keepdims=True)
        rows.append(m)
        v = jnp.where(v == m, -jnp.inf, v)
    return rows


def _peer_topk_kernel(q_ref, keys_ref, st_ref, tau_ref, c_ref, *, half):
    tops = []
    for hp in range(2 * PEER_HEADS):
        st = _nt_dot(keys_ref[hp], q_ref[:, hp * half:(hp + 1) * half].astype(BF16))
        st_ref[hp * PEER_KEYS:(hp + 1) * PEER_KEYS, :] = st
        tops.append(_extract_top(st, PEER_TOPK))
    for hd in range(PEER_HEADS):
        a_rows, b_rows = tops[2 * hd], tops[2 * hd + 1]
        cand = [a_rows[i] + b_rows[j] for i in range(PEER_TOPK) for j in range(PEER_TOPK // (i + 1))]
        pad = (-len(cand)) % 8
        cand = jnp.concatenate(cand + [jnp.full_like(cand[0], -jnp.inf)] * pad, axis=0)
        best = _extract_top(cand, PEER_TOPK)
        tau_ref[hd:hd + 1, :] = best[PEER_TOPK - 1]
        tot = jnp.exp(best[0] - best[0])
        for bi in best[1:]:
            tot += jnp.exp(bi - best[0])
        c_ref[hd:hd + 1, :] = best[0] + jnp.log(tot)


def peer_topk(qp, sub_keys):
    T = qp.shape[0]
    half = sub_keys.shape[-1]
    tm = _tile(T, 512)
    rows = 2 * PEER_HEADS * PEER_KEYS
    keys = sub_keys.reshape(2 * PEER_HEADS, PEER_KEYS, half).astype(BF16)
    return pl.pallas_call(
        functools.partial(_peer_topk_kernel, half=half),
        grid=(T // tm,),
        in_specs=[pl.BlockSpec((tm, qp.shape[1]), lambda i: (i, 0)),
                  pl.BlockSpec(keys.shape, lambda i: (0, 0, 0))],
        out_specs=[pl.BlockSpec((rows, tm), lambda i: (0, i)),
                   pl.BlockSpec((PEER_HEADS, tm), lambda i: (0, i)),
                   pl.BlockSpec((PEER_HEADS, tm), lambda i: (0, i))],
        out_shape=[jax.ShapeDtypeStruct((rows, T), F32), jax.ShapeDtypeStruct((PEER_HEADS, T), F32),
                   jax.ShapeDtypeStruct((PEER_HEADS, T), F32)],
        compiler_params=_cparams("parallel"),
    )(qp, keys)


def _peer_kernel(xn_ref, down_ref, up_ref, st_ref, tau_ref, c_ref, o_ref, *, tm, tn):
    j = pl.program_id(1)

    @pl.when(j == 0)
    def _():
        o_ref[...] = jnp.zeros(o_ref.shape, F32)

    na = tn // PEER_KEYS
    th = tm // 2
    toks = [slice(half * th, (half + 1) * th) for half in range(2)]

    def build_gate(tok):
        gates = []
        for ai in range(na):
            g = jnp.zeros((PEER_KEYS, th), F32)
            for hd in range(PEER_HEADS):
                s0a = st_ref[pl.ds(2 * hd * PEER_KEYS + j * na + ai, 1), tok]
                s1 = st_ref[(2 * hd + 1) * PEER_KEYS:(2 * hd + 2) * PEER_KEYS, tok]
                z = s0a + s1
                g += jnp.where(z >= tau_ref[hd:hd + 1, tok], jnp.exp(z - c_ref[hd:hd + 1, tok]), 0.0)
            gates.append(g)
        return (gates[0] if na == 1 else jnp.concatenate(gates, axis=0)).T

    gate0 = build_gate(toks[0])
    hids = [_nt_dot(xn_ref[tok, :], down_ref[...]) for tok in toks]
    gate1 = build_gate(toks[1])
    for tok, hid, gate in zip(toks, hids, (gate0, gate1)):
        act = 0.5 * hid * (1.0 + lax.erf(hid * (2.0 ** -0.5)))
        o_ref[tok, :] += jnp.dot((gate * act).astype(BF16), up_ref[...], preferred_element_type=F32)


def peer_dense(xn, down, up, st, tau, c):
    T, D = xn.shape
    NE = down.shape[0]
    tm, tn = _tile(T, 512), _tile(NE, 256)
    return pl.pallas_call(
        functools.partial(_peer_kernel, tm=tm, tn=tn),
        grid=(T // tm, NE // tn),
        in_specs=[pl.BlockSpec((tm, D), lambda i, j: (i, 0)),
                  pl.BlockSpec((tn, D), lambda i, j: (j, 0)),
                  pl.BlockSpec((tn, D), lambda i, j: (j, 0)),
                  pl.BlockSpec((st.shape[0], tm), lambda i, j: (0, i)),
                  pl.BlockSpec((PEER_HEADS, tm), lambda i, j: (0, i)),
                  pl.BlockSpec((PEER_HEADS, tm), lambda i, j: (0, i))],
        out_specs=pl.BlockSpec((tm, D), lambda i, j: (i, 0)),
        out_shape=jax.ShapeDtypeStruct((T, D), F32),
        compiler_params=_cparams("parallel", "arbitrary"),
    )(xn, down, up, st, tau, c)


def _pad_cols(w, mult):
    pad = (-w.shape[1]) % mult
    return jnp.pad(w, ((0, 0), (0, pad))) if pad else w


def kernel(x, norm_mix_w, w_in, da_q_norm_w, da_k_norm_w, da_lambda, da_sub_norm_w, rw_mu, rw_w0, rw_w2, rw_a0, rw_a2, rw_g2, rw_k_k, rw_k_a, rw_r_k, rw_ln_w, rw_ln_b, ssm_conv_w, ssm_conv_b, ssm_dt_bias, ssm_a_log, ssm_d, ssm_norm_w, w_branch, w_out, norm_ffn_w, peer_w_query, peer_sub_keys, peer_down, peer_up):
    B, S, D = x.shape
    T = B * S
    depth = w_in.shape[0]
    da_heads = D // 512
    da_cols = 2 * da_heads * HEAD_DIM
    da_width = da_heads * 2 * HEAD_DIM
    rw_heads = D // 256
    rw_width = rw_heads * HEAD_DIM
    rw_sizes = (rw_width, rw_width, rw_width) + RW_LORA
    rw_cols = sum(rw_sizes)
    ssm_heads = D // 128
    ssm_width = ssm_heads * HEAD_DIM
    ssm_xbc = ssm_width + 2 * SSM_GROUPS * SSM_STATE
    sizes = (3 * D, da_cols, da_cols, da_width, rw_cols, ssm_width, ssm_xbc, ssm_heads)
    offs = [0]
    for sz in sizes:
        offs.append(offs[-1] + sz)
    assert offs[-1] == w_in.shape[2]
    c_q, c_k, c_v, c_rw, c_z, c_xbc, c_dt = offs[1:8]

    pos = jnp.arange(S, dtype=F32)
    inv_freq = ROPE_THETA ** (-jnp.arange(0, HEAD_DIM, 2, dtype=F32) / HEAD_DIM)
    ang = pos[:, None] * inv_freq[None, :]
    cos, sin = jnp.cos(ang), jnp.sin(ang)

    xt = x.reshape(T, D)
    for l in range(depth):
        lambda_init = 0.8 - 0.6 * math.exp(-0.3 * l)
        xn = rmsnorm_bf16(xt, norm_mix_w[l])
        proj = matmul(xn, _pad_cols(w_in[l].astype(BF16), 512))

        q = proj[:, c_q:c_q + da_cols].reshape(B, S, 2 * da_heads, HEAD_DIM)
        k = proj[:, c_k:c_k + da_cols].reshape(B, S, 2 * da_heads, HEAD_DIM)
        q = _rotary(_head_rms(q, da_q_norm_w[l]), cos, sin) * (HEAD_DIM ** -0.5)
        k = _rotary(_head_rms(k, da_k_norm_w[l]), cos, sin)
        v = proj[:, c_v:c_v + da_width].reshape(B, S, da_width)
        lv = da_lambda[l]
        lam = jnp.exp(jnp.sum(lv[0] * lv[1])) - jnp.exp(jnp.sum(lv[2] * lv[3])) + lambda_init
        o_da = diff_attention(lam.reshape(1), q.reshape(B, S, da_cols).astype(BF16),
                              k.reshape(B, S, da_cols).astype(BF16), v.astype(BF16), da_sub_norm_w[l], lambda_init)
        o_da = o_da.reshape(T, da_width)

        p = proj[:, c_rw:c_rw + rw_cols].reshape(B, S, rw_cols)
        p_prev = jnp.pad(p, ((0, 0), (1, 0), (0, 0)))[:, :-1]
        p = p + (p_prev - p) * rw_mu[l]
        r_, k_, v_, w_lo, a_lo, g_lo = jnp.split(p, [sum(rw_sizes[:n]) for n in range(1, len(rw_sizes))], axis=-1)
        w_ = -jax.nn.softplus(-(rw_w0[l] + jnp.tanh(w_lo) @ rw_w2[l])) - 0.5
        logw = -jnp.exp(w_)
        a_ = jax.nn.sigmoid(rw_a0[l] + a_lo @ rw_a2[l])
        g_ = jax.nn.sigmoid(g_lo) @ rw_g2[l]
        heads = lambda t: t.reshape(B, S, rw_heads, HEAD_DIM).transpose(0, 2, 1, 3)
        kk = heads(k_ * rw_k_k[l])
        kk = kk * lax.rsqrt(jnp.maximum(jnp.sum(kk * kk, axis=-1, keepdims=True), 1e-24))
        k_ = k_ * (1.0 + (a_ - 1.0) * rw_k_a[l])
        rh, kh, vh, ah = heads(r_), heads(k_), heads(v_), heads(a_)
        y = rwkv_scan(rh, heads(logw), kh, vh, -kk, kk * ah)
        mean = jnp.mean(y, axis=-1, keepdims=True)
        var = jnp.mean(jnp.square(y - mean), axis=-1, keepdims=True)
        y = (y - mean) * lax.rsqrt(var + RW_GN_EPS)
        y = y * rw_ln_w[l].reshape(1, rw_heads, 1, HEAD_DIM) + rw_ln_b[l].reshape(1, rw_heads, 1, HEAD_DIM)
        y = y + jnp.sum(rh * kh * rw_r_k[l][None, :, None, :], axis=-1, keepdims=True) * vh
        o_rw = (y.transpose(0, 2, 1, 3).reshape(B, S, rw_width) * g_).astype(BF16).reshape(T, rw_width)

        xbc = proj[:, c_xbc:c_xbc + ssm_xbc].reshape(B, S, ssm_xbc)
        xpad = jnp.pad(xbc, ((0, 0), (SSM_CONV - 1, 0), (0, 0)))
        conv = ssm_conv_b[l] + sum(ssm_conv_w[l][jj] * xpad[:, jj:jj + S] for jj in range(SSM_CONV))
        xbc = conv * jax.nn.sigmoid(conv)
        dt = jax.nn.softplus(proj[:, c_dt:c_dt + ssm_heads].reshape(B, S, ssm_heads) + ssm_dt_bias[l])
        o_ssm = ssd_scan(xbc, proj[:, c_z:c_z + ssm_width], dt, -jnp.exp(ssm_a_log[l]), ssm_d[l], ssm_norm_w[l],
                         B, S, ssm_heads)

        wb = w_branch[l].astype(BF16)
        merged = merge_branches(proj, o_da, o_rw, o_ssm, wb[:da_width], wb[da_width:da_width + rw_width],
                                wb[da_width + rw_width:], D)
        xt = matmul(merged, w_out[l].astype(BF16), residual=xt)

        xn2 = rmsnorm_bf16(xt, norm_ffn_w[l])
        qp = matmul(xn2, peer_w_query[l].astype(BF16))
        st, tau, c = peer_topk(qp, peer_sub_keys[l])
        yp = peer_dense(xn2, peer_down[l].astype(BF16), peer_up[l].astype(BF16), st, tau, c)
        xt = xt + yp
    return xt.reshape(B, S, D)
```

```python
import functools
import math

import jax
import jax.numpy as jnp
from jax import lax
from jax.experimental import pallas as pl
from jax.experimental.pallas import tpu as pltpu

F32 = jnp.float32
BF16 = jnp.bfloat16
HIGHEST = lax.Precision.HIGHEST

LANES = 128
VMEM_LIMIT_BYTES = 56 * 1024 * 1024

HEAD_DIM = 64
RMS_EPS = 1e-6
ROPE_THETA = 10000.0
RW_LORA = (64, 64, 128)
RW_GN_EPS = 64e-5
RW_CHUNK = 64
SSM_GROUPS = 4
SSM_STATE = 128
SSM_CONV = 4
SSM_CHUNK = 256
SSM_NORM_EPS = 1e-5
PEER_HEADS = 8
PEER_KEYS = 128
PEER_TOPK = 16


def _cparams(*sem):
    return pltpu.CompilerParams(dimension_semantics=sem, vmem_limit_bytes=VMEM_LIMIT_BYTES)


def _tile(n, pref):
    t = min(n, pref)
    assert n % t == 0, (n, pref)
    return t


def _nt_dot(a, b, **kw):
    return lax.dot_general(a, b, (((1,), (1,)), ((), ())), preferred_element_type=F32, **kw)


def _tn_dot(a, b, **kw):
    return lax.dot_general(a, b, (((0,), (0,)), ((), ())), preferred_element_type=F32, **kw)


def _rmsnorm_kernel(x_ref, w_ref, o_ref):
    x = x_ref[...]
    ms = jnp.mean(x * x, axis=-1, keepdims=True)
    o_ref[...] = (x * lax.rsqrt(ms + RMS_EPS) * w_ref[...]).astype(o_ref.dtype)


def rmsnorm_bf16(x, w):
    T, D = x.shape
    tm = _tile(T, 256)
    return pl.pallas_call(
        _rmsnorm_kernel,
        grid=(T // tm,),
        in_specs=[pl.BlockSpec((tm, D), lambda i: (i, 0)), pl.BlockSpec((1, D), lambda i: (0, 0))],
        out_specs=pl.BlockSpec((tm, D), lambda i: (i, 0)),
        out_shape=jax.ShapeDtypeStruct((T, D), BF16),
        compiler_params=_cparams("parallel"),
    )(x, w.reshape(1, D))


def _mm_kernel(a_ref, b_ref, o_ref):
    o_ref[...] = jnp.dot(a_ref[...], b_ref[...], preferred_element_type=F32).astype(o_ref.dtype)


def _mm_res_kernel(a_ref, b_ref, r_ref, o_ref):
    o_ref[...] = r_ref[...] + jnp.dot(a_ref[...], b_ref[...], preferred_element_type=F32)


def matmul(a, b, residual=None, out_dtype=F32, tm_pref=1024, tn_pref=512):
    M, K = a.shape
    N = b.shape[1]
    tm, tn = _tile(M, tm_pref), _tile(N, tn_pref)
    in_specs = [pl.BlockSpec((tm, K), lambda i, j: (i, 0)), pl.BlockSpec((K, tn), lambda i, j: (0, j))]
    args = [a, b]
    kern = _mm_kernel
    if residual is not None:
        in_specs.append(pl.BlockSpec((tm, tn), lambda i, j: (i, j)))
        args.append(residual)
        kern = _mm_res_kernel
    return pl.pallas_call(
        kern,
        grid=(M // tm, N // tn),
        in_specs=in_specs,
        out_specs=pl.BlockSpec((tm, tn), lambda i, j: (i, j)),
        out_shape=jax.ShapeDtypeStruct((M, N), out_dtype),
        compiler_params=_cparams("parallel", "arbitrary"),
    )(*args)


def _merge_kernel(g0_ref, g1_ref, g2_ref, a0_ref, a1_ref, a2_ref, w0_ref, w1_ref, w2_ref, o_ref):
    acc = jax.nn.sigmoid(g0_ref[...]) * jnp.dot(a0_ref[...], w0_ref[...], preferred_element_type=F32)
    acc += jax.nn.sigmoid(g1_ref[...]) * jnp.dot(a1_ref[...], w1_ref[...], preferred_element_type=F32)
    acc += jax.nn.sigmoid(g2_ref[...]) * jnp.dot(a2_ref[...], w2_ref[...], preferred_element_type=F32)
    o_ref[...] = acc.astype(o_ref.dtype)


def merge_branches(proj, o_da, o_rw, o_ssm, wb_da, wb_rw, wb_ssm, D):
    T = proj.shape[0]
    tm, tn = _tile(T, 512), _tile(D, 512)
    nb = D // tn

    def gate_spec(n):
        return pl.BlockSpec((tm, tn), lambda i, j: (i, n * nb + j))

    def act_spec(a):
        return pl.BlockSpec((tm, a.shape[1]), lambda i, j: (i, 0))

    def w_spec(w):
        return pl.BlockSpec((w.shape[0], tn), lambda i, j: (0, j))

    return pl.pallas_call(
        _merge_kernel,
        grid=(T // tm, nb),
        in_specs=[gate_spec(0), gate_spec(1), gate_spec(2), act_spec(o_da), act_spec(o_rw), act_spec(o_ssm),
                  w_spec(wb_da), w_spec(wb_rw), w_spec(wb_ssm)],
        out_specs=pl.BlockSpec((tm, tn), lambda i, j: (i, j)),
        out_shape=jax.ShapeDtypeStruct((T, D), BF16),
        compiler_params=_cparams("parallel", "arbitrary"),
    )(proj, proj, proj, o_da, o_rw, o_ssm, wb_da, wb_rw, wb_ssm)


def _diff_attn_kernel(lam_ref, qt_ref, k_ref, vt_ref, nw_ref, o_ref, m_ref, l_ref, acc_ref, *, tq, hp, out_scale):
    qi = pl.program_id(2)
    qmaps = []
    for h in range(hp):
        qt = qt_ref[0, h, 0]
        dim = lax.broadcasted_iota(jnp.int32, qt.shape, 0)
        zero = jnp.zeros_like(qt)
        qmaps += [jnp.where(dim < HEAD_DIM, qt, zero), jnp.where(dim >= HEAD_DIM, qt, zero)]
    m_ref[...] = jnp.full(m_ref.shape, -jnp.inf, F32)
    l_ref[...] = jnp.zeros(l_ref.shape, F32)
    acc_ref[...] = jnp.zeros(acc_ref.shape, F32)

    def step(j, masked):
        kt = k_ref[0, pl.ds(j * tq, tq), :]
        scores = [jnp.dot(kt[:, (c // 2) * LANES:(c // 2 + 1) * LANES], qc, preferred_element_type=F32)
                  for c, qc in enumerate(qmaps)]
        for c, s in enumerate(scores):
            vt = vt_ref[0, c // 2, j]
            if masked:
                key = lax.broadcasted_iota(jnp.int32, (tq, tq), 0)
                qry = lax.broadcasted_iota(jnp.int32, (tq, tq), 1)
                s = jnp.where(key <= qry, s, -jnp.inf)
            m_old = m_ref[c]
            m_new = jnp.maximum(m_old, jnp.max(s, axis=0, keepdims=True))
            alpha = jnp.exp(m_old - m_new)
            p = jnp.exp(s - m_new)
            l_ref[c] = alpha * l_ref[c] + jnp.sum(p, axis=0, keepdims=True)
            acc_ref[c] = alpha * acc_ref[c] + jnp.dot(vt, p.astype(vt.dtype), preferred_element_type=F32)
            m_ref[c] = m_new

    def body(j, c):
        step(j, False)
        return c

    lax.fori_loop(0, qi, body, 0)
    step(qi, True)
    for h in range(hp):
        o = acc_ref[2 * h] / l_ref[2 * h] - lam_ref[0] * (acc_ref[2 * h + 1] / l_ref[2 * h + 1])
        ms = jnp.mean(o * o, axis=0, keepdims=True)
        o = o * lax.rsqrt(ms + RMS_EPS) * nw_ref[...] * out_scale
        o_ref[0, :, h * LANES:(h + 1) * LANES] = o.T.astype(o_ref.dtype)


def diff_attention(lam, q, k, v, sub_norm_w, lambda_init):
    B, S, W = q.shape
    H = W // LANES
    tq = _tile(S, 512)
    nq = S // tq
    hp = _tile(H, 2)
    to_t = lambda t: t.reshape(B, nq, tq, H, LANES).transpose(0, 3, 1, 4, 2)
    kern = functools.partial(_diff_attn_kernel, tq=tq, hp=hp, out_scale=1.0 - lambda_init)
    return pl.pallas_call(
        kern,
        grid=(B, H // hp, nq),
        in_specs=[pl.BlockSpec(memory_space=pltpu.SMEM),
                  pl.BlockSpec((1, hp, 1, LANES, tq), lambda b, h, i: (b, h, i, 0, 0)),
                  pl.BlockSpec((1, S, hp * LANES), lambda b, h, i: (b, 0, h)),
                  pl.BlockSpec((1, hp, nq, LANES, tq), lambda b, h, i: (b, h, 0, 0, 0)),
                  pl.BlockSpec((LANES, 1), lambda b, h, i: (0, 0))],
        out_specs=pl.BlockSpec((1, tq, hp * LANES), lambda b, h, i: (b, i, h)),
        out_shape=jax.ShapeDtypeStruct((B, S, W), BF16),
        scratch_shapes=[pltpu.VMEM((2 * hp, 1, tq), F32), pltpu.VMEM((2 * hp, 1, tq), F32),
                        pltpu.VMEM((2 * hp, LANES, tq), F32)],
        compiler_params=_cparams("parallel", "parallel", "arbitrary"),
    )(lam, to_t(q), k, to_t(v), sub_norm_w.reshape(LANES, 1))


def _rotary(x, cos, sin):
    x1, x2 = jnp.split(x, 2, axis=-1)
    c, s = cos[None, :, None, :], sin[None, :, None, :]
    return jnp.concatenate([x1 * c - x2 * s, x2 * c + x1 * s], axis=-1)


def _head_rms(x, w):
    return x * lax.rsqrt(jnp.mean(x * x, axis=-1, keepdims=True) + RMS_EPS) * w


def _rwkv_chunk_kernel(r_ref, lw_ref, k_ref, v_ref, a_ref, b_ref, y_ref, state_ref, *, hb, C):
    @pl.when(pl.program_id(2) == 0)
    def _():
        state_ref[...] = jnp.zeros(state_ref.shape, F32)

    ti = lax.broadcasted_iota(jnp.int32, (C, C), 0)
    si = lax.broadcasted_iota(jnp.int32, (C, C), 1)
    incl = si <= ti
    strict = si < ti
    tri = incl.astype(F32)
    bf = lambda t: t.astype(BF16)
    dot = lambda x, y: jnp.dot(bf(x), bf(y), preferred_element_type=F32)
    hs = range(hb)
    r, lw, k, v, a, b = ([ref[0, h] for h in hs] for ref in (r_ref, lw_ref, k_ref, v_ref, a_ref, b_ref))
    cum = [jnp.dot(tri, lw[h], preferred_element_type=F32, precision=HIGHEST) for h in hs]
    ar = [bf(jnp.concatenate([a[h] * jnp.exp(cum[h] - lw[h]), r[h] * jnp.exp(cum[h])], axis=0)) for h in hs]
    inv = [jnp.exp(-cum[h]) for h in hs]
    bk = [bf(jnp.concatenate([b[h] * inv[h], k[h] * inv[h]], axis=0)) for h in hs]
    g = [_nt_dot(ar[h], bk[h]) for h in hs]
    s0 = [state_ref[h] for h in hs]
    x = [_nt_dot(ar[h], bf(s0[h])) for h in hs]
    u = [x[h][:C] + dot(jnp.where(strict, g[h][:C, C:], 0.0), v[h]) for h in hs]
    lp = [jnp.where(strict, g[h][:C, :C], 0.0) for h in hs]
    n_double = max(1, (C - 1).bit_length())
    for it in range(n_double):
        u = [u[h] + dot(lp[h], u[h]) for h in hs]
        if it + 1 < n_double:
            lp = [dot(lp[h], lp[h]) for h in hs]
    for h in hs:
        m_rb = jnp.where(incl, g[h][C:, :C], 0.0)
        m_rk = jnp.where(incl, g[h][C:, C:], 0.0)
        y_ref[0, h] = x[h][C:] + dot(m_rb, u[h]) + dot(m_rk, v[h])
    for h in hs:
        to_end = jnp.exp(cum[h][C - 1:C, :] - cum[h])
        uv = jnp.concatenate([u[h], v[h]], axis=0)
        bk_end = jnp.concatenate([b[h] * to_end, k[h] * to_end], axis=0)
        state_ref[h] = s0[h] * jnp.exp(cum[h][C - 1:C, :]) + _tn_dot(bf(uv), bf(bk_end))


def rwkv_scan(r, lw, k, v, a, b):
    B, H, S, N = r.shape
    C = _tile(S, RW_CHUNK)
    hb = _tile(H, 8)
    spec = pl.BlockSpec((1, hb, C, N), lambda bi, hi, ci: (bi, hi, ci, 0))
    return pl.pallas_call(
        functools.partial(_rwkv_chunk_kernel, hb=hb, C=C),
        grid=(B, H // hb, S // C),
        in_specs=[spec] * 6,
        out_specs=spec,
        out_shape=jax.ShapeDtypeStruct((B, H, S, N), F32),
        scratch_shapes=[pltpu.VMEM((hb, N, N), F32)],
        compiler_params=_cparams("parallel", "parallel", "arbitrary"),
    )(r, lw, k, v, a, b)


def _ssd_kernel(xs_ref, bm_ref, cm_ref, z_ref, dt_ref, dtt_ref, a_ref, acol_ref, d_ref, nw_ref, ex_ref, o_ref,
                state_ref, *, L, E, P):
    @pl.when(pl.program_id(2) == 0)
    def _():
        state_ref[...] = jnp.zeros(state_ref.shape, F32)

    li = lax.broadcasted_iota(jnp.int32, (L, L), 0)
    si = lax.broadcasted_iota(jnp.int32, (L, L), 1)
    lower = si <= li
    hdot = functools.partial(jnp.dot, preferred_element_type=F32, precision=HIGHEST)
    ex = ex_ref[...]
    a_row = a_ref[0]
    dt = dt_ref[0, 0]
    cs = hdot(lower.astype(F32), dt * a_row)
    cs_row = hdot(dtt_ref[0, 0] * acol_ref[0], (li <= si).astype(F32))
    cs_last = cs[L - 1:L, :]
    xs = xs_ref[0]
    xdt = xs * hdot(dt, ex)
    bm = bm_ref[0].astype(BF16)
    cm = cm_ref[0].astype(BF16)
    cb = _nt_dot(cm, bm)
    state_in = state_ref[...]
    y = hdot(jnp.exp(cs), ex) * jnp.dot(cm, state_in.astype(BF16), preferred_element_type=F32)
    y += d_ref[0] * xs
    lane = lax.broadcasted_iota(jnp.int32, (L, 2 * P), 1)
    ydiag = []
    for pr in range(E // 2):
        xpair = xdt[:, pr * 2 * P:(pr + 1) * 2 * P]
        acc = jnp.zeros((L, 2 * P), F32)
        for sub in range(2):
            e = 2 * pr + sub
            seg = jnp.where(lower, jnp.exp(jnp.minimum(cs[:, e:e + 1] - cs_row[e:e + 1, :], 0.0)), 0.0)
            xm = jnp.where((lane >= sub * P) & (lane < (sub + 1) * P), xpair, 0.0)
            acc += jnp.dot((cb * seg).astype(BF16), xm.astype(BF16), preferred_element_type=F32)
        ydiag.append(acc)
    y += jnp.concatenate(ydiag, axis=1)
    xend = (xdt * hdot(jnp.exp(cs_last - cs), ex)).astype(BF16)
    state_ref[...] = hdot(jnp.exp(cs_last), ex) * state_in + _tn_dot(bm, xend)
    zg = z_ref[...]
    y = y * (zg * jax.nn.sigmoid(zg))
    y = y * lax.rsqrt(jnp.mean(y * y, axis=-1, keepdims=True) + SSM_NORM_EPS)
    o_ref[...] = (y * nw_ref[...]).astype(o_ref.dtype)


def ssd_scan(xbc, z, dt, a_neg, d_skip, norm_w, B, S, H):
    G, N, P = SSM_GROUPS, SSM_STATE, HEAD_DIM
    E = H // G
    EP = E * P
    W = H * P
    L = math.gcd(S, SSM_CHUNK)
    nc = S // L
    dt_g = dt.reshape(B, S, G, E).transpose(0, 2, 1, 3)
    dt_gt = dt_g.transpose(0, 1, 3, 2)
    ex = (jnp.arange(EP)[None, :] // P == jnp.arange(E)[:, None]).astype(F32)
    d_full = jnp.repeat(d_skip.reshape(G, 1, E), P, axis=2)
    kern = functools.partial(_ssd_kernel, L=L, E=E, P=P)
    return pl.pallas_call(
        kern,
        grid=(B, G, nc),
        in_specs=[pl.BlockSpec((1, L, EP), lambda b, g, c: (b, c, g)),
                  pl.BlockSpec((1, L, N), lambda b, g, c: (b, c, W // N + g)),
                  pl.BlockSpec((1, L, N), lambda b, g, c: (b, c, W // N + G + g)),
                  pl.BlockSpec((L, EP), lambda b, g, c: (b * nc + c, g)),
                  pl.BlockSpec((1, 1, L, E), lambda b, g, c: (b, g, c, 0)),
                  pl.BlockSpec((1, 1, E, L), lambda b, g, c: (b, g, 0, c)),
                  pl.BlockSpec((1, 1, E), lambda b, g, c: (g, 0, 0)),
                  pl.BlockSpec((1, E, 1), lambda b, g, c: (g, 0, 0)),
                  pl.BlockSpec((1, 1, EP), lambda b, g, c: (g, 0, 0)),
                  pl.BlockSpec((1, EP), lambda b, g, c: (0, g)),
                  pl.BlockSpec((E, EP), lambda b, g, c: (0, 0))],
        out_specs=pl.BlockSpec((L, EP), lambda b, g, c: (b * nc + c, g)),
        out_shape=jax.ShapeDtypeStruct((B * S, W), BF16),
        scratch_shapes=[pltpu.VMEM((N, EP), F32)],
        compiler_params=_cparams("parallel", "parallel", "arbitrary"),
    )(xbc, xbc, xbc, z, dt_g, dt_gt, a_neg.reshape(G, 1, E), a_neg.reshape(G, E, 1), d_full,
      norm_w.reshape(1, W), ex)


def _extract_top(v, n):
    rows = []
    for _ in range(n):
        m = jnp.max(v, axis=0, keepdims=True)
        rows.append(m)
        v = jnp.where(v == m, -jnp.inf, v)
    return rows


def _peer_topk_kernel(q_ref, keys_ref, st_ref, tau_ref, c_ref, *, half):
    tops = []
    for hp in range(2 * PEER_HEADS):
        st = _nt_dot(keys_ref[hp], q_ref[:, hp * half:(hp + 1) * half].astype(BF16))
        st_ref[hp * PEER_KEYS:(hp + 1) * PEER_KEYS, :] = st
        tops.append(_extract_top(st, PEER_TOPK))
    for hd in range(PEER_HEADS):
        a_rows, b_rows = tops[2 * hd], tops[2 * hd + 1]
        cand = [a_rows[i] + b_rows[j] for i in range(PEER_TOPK) for j in range(PEER_TOPK // (i + 1))]
        pad = (-len(cand)) % 8
        cand = jnp.concatenate(cand + [jnp.full_like(cand[0], -jnp.inf)] * pad, axis=0)
        best = _extract_top(cand, PEER_TOPK)
        tau_ref[hd:hd + 1, :] = best[PEER_TOPK - 1]
        tot = jnp.exp(best[0] - best[0])
        for bi in best[1:]:
            tot += jnp.exp(bi - best[0])
        c_ref[hd:hd + 1, :] = best[0] + jnp.log(tot)


def peer_topk(qp, sub_keys):
    T = qp.shape[0]
    half = sub_keys.shape[-1]
    tm = _tile(T, 512)
    rows = 2 * PEER_HEADS * PEER_KEYS
    keys = sub_keys.reshape(2 * PEER_HEADS, PEER_KEYS, half).astype(BF16)
    return pl.pallas_call(
        functools.partial(_peer_topk_kernel, half=half),
        grid=(T // tm,),
        in_specs=[pl.BlockSpec((tm, qp.shape[1]), lambda i: (i, 0)),
                  pl.BlockSpec(keys.shape, lambda i: (0, 0, 0))],
        out_specs=[pl.BlockSpec((rows, tm), lambda i: (0, i)),
                   pl.BlockSpec((PEER_HEADS, tm), lambda i: (0, i)),
                   pl.BlockSpec((PEER_HEADS, tm), lambda i: (0, i))],
        out_shape=[jax.ShapeDtypeStruct((rows, T), F32), jax.ShapeDtypeStruct((PEER_HEADS, T), F32),
                   jax.ShapeDtypeStruct((PEER_HEADS, T), F32)],
        compiler_params=_cparams("parallel"),
    )(qp, keys)


def _peer_kernel(xn_ref, down_ref, up_ref, st_ref, tau_ref, c_ref, res_ref, o_ref, w_ref, hid_ref, *, tm, tn, nj):
    j = pl.program_id(1)
    na = tn // PEER_KEYS

    @pl.when(j == 0)
    def _():
        o_ref[...] = res_ref[...]
        w_ref[1] = jnp.zeros(w_ref.shape[1:], w_ref.dtype)

    jt = jnp.minimum(j, nj - 1)

    def run(rd, wr):
        s0rows = [[st_ref[pl.ds(2 * hd * PEER_KEYS + jt * na + ai, 1), :] for hd in range(PEER_HEADS)]
                  for ai in range(na)]
        o_ref[...] += jnp.dot(w_ref[rd], up_ref[...], preferred_element_type=F32)
        hid_ref[...] = _nt_dot(xn_ref[...], down_ref[...])
        for tb in range(tm // LANES):
            tok = slice(tb * LANES, (tb + 1) * LANES)
            for ai in range(na):
                exp_cols = slice(ai * PEER_KEYS, (ai + 1) * PEER_KEYS)
                g = jnp.zeros((PEER_KEYS, LANES), F32)
                for hd in range(PEER_HEADS):
                    s1 = st_ref[(2 * hd + 1) * PEER_KEYS:(2 * hd + 2) * PEER_KEYS, tok]
                    z = s0rows[ai][hd][:, tok] + s1
                    g += jnp.where(z >= tau_ref[hd:hd + 1, tok], jnp.exp(z - c_ref[hd:hd + 1, tok]), 0.0)
                hid = hid_ref[tok, exp_cols]
                act = 0.5 * hid * (1.0 + lax.erf(hid * (2.0 ** -0.5)))
                w_ref[wr, tok, exp_cols] = (g.T * act).astype(w_ref.dtype)

    @pl.when(j % 2 == 0)
    def _():
        run(1, 0)

    @pl.when(j % 2 == 1)
    def _():
        run(0, 1)


def peer_dense(xn, down, up, st, tau, c, res):
    T, D = xn.shape
    NE = down.shape[0]
    tm, tn = _tile(T, 512), _tile(NE, 256)
    nj = NE // tn
    once = pl.Buffered(1)
    return pl.pallas_call(
        functools.partial(_peer_kernel, tm=tm, tn=tn, nj=nj),
        grid=(T // tm, nj + 1),
        in_specs=[pl.BlockSpec((tm, D), lambda i, j: (i, 0), pipeline_mode=once),
                  pl.BlockSpec((tn, D), lambda i, j: (jnp.minimum(j, nj - 1), 0)),
                  pl.BlockSpec((tn, D), lambda i, j: (jnp.maximum(j - 1, 0), 0)),
                  pl.BlockSpec((st.shape[0], tm), lambda i, j: (0, i), pipeline_mode=once),
                  pl.BlockSpec((PEER_HEADS, tm), lambda i, j: (0, i)),
                  pl.BlockSpec((PEER_HEADS, tm), lambda i, j: (0, i)),
                  pl.BlockSpec((tm, D), lambda i, j: (i, 0), pipeline_mode=once)],
        out_specs=pl.BlockSpec((tm, D), lambda i, j: (i, 0)),
        out_shape=jax.ShapeDtypeStruct((T, D), F32),
        scratch_shapes=[pltpu.VMEM((2, tm, tn), BF16), pltpu.VMEM((tm, tn), F32)],
        compiler_params=_cparams("parallel", "arbitrary"),
    )(xn, down, up, st, tau, c, res)


def _pad_cols(w, mult):
    pad = (-w.shape[1]) % mult
    return jnp.pad(w, ((0, 0), (0, pad))) if pad else w


def kernel(x, norm_mix_w, w_in, da_q_norm_w, da_k_norm_w, da_lambda, da_sub_norm_w, rw_mu, rw_w0, rw_w2, rw_a0, rw_a2, rw_g2, rw_k_k, rw_k_a, rw_r_k, rw_ln_w, rw_ln_b, ssm_conv_w, ssm_conv_b, ssm_dt_bias, ssm_a_log, ssm_d, ssm_norm_w, w_branch, w_out, norm_ffn_w, peer_w_query, peer_sub_keys, peer_down, peer_up):
    B, S, D = x.shape
    T = B * S
    depth = w_in.shape[0]
    da_heads = D // 512
    da_cols = 2 * da_heads * HEAD_DIM
    da_width = da_heads * 2 * HEAD_DIM
    rw_heads = D // 256
    rw_width = rw_heads * HEAD_DIM
    rw_sizes = (rw_width, rw_width, rw_width) + RW_LORA
    rw_cols = sum(rw_sizes)
    ssm_heads = D // 128
    ssm_width = ssm_heads * HEAD_DIM
    ssm_xbc = ssm_width + 2 * SSM_GROUPS * SSM_STATE
    sizes = (3 * D, da_cols, da_cols, da_width, rw_cols, ssm_width, ssm_xbc, ssm_heads)
    offs = [0]
    for sz in sizes:
        offs.append(offs[-1] + sz)
    assert offs[-1] == w_in.shape[2]
    c_q, c_k, c_v, c_rw, c_z, c_xbc, c_dt = offs[1:8]

    pos = jnp.arange(S, dtype=F32)
    inv_freq = ROPE_THETA ** (-jnp.arange(0, HEAD_DIM, 2, dtype=F32) / HEAD_DIM)
    ang = pos[:, None] * inv_freq[None, :]
    cos, sin = jnp.cos(ang), jnp.sin(ang)

    xt = x.reshape(T, D)
    for l in range(depth):
        lambda_init = 0.8 - 0.6 * math.exp(-0.3 * l)
        xn = rmsnorm_bf16(xt, norm_mix_w[l])
        proj = matmul(xn, _pad_cols(w_in[l].astype(BF16), 512))

        q = proj[:, c_q:c_q + da_cols].reshape(B, S, 2 * da_heads, HEAD_DIM)
        k = proj[:, c_k:c_k + da_cols].reshape(B, S, 2 * da_heads, HEAD_DIM)
        q = _rotary(_head_rms(q, da_q_norm_w[l]), cos, sin) * (HEAD_DIM ** -0.5)
        k = _rotary(_head_rms(k, da_k_norm_w[l]), cos, sin)
        v = proj[:, c_v:c_v + da_width].reshape(B, S, da_width)
        lv = da_lambda[l]
        lam = jnp.exp(jnp.sum(lv[0] * lv[1])) - jnp.exp(jnp.sum(lv[2] * lv[3])) + lambda_init
        o_da = diff_attention(lam.reshape(1), q.reshape(B, S, da_cols).astype(BF16),
                              k.reshape(B, S, da_cols).astype(BF16), v.astype(BF16), da_sub_norm_w[l], lambda_init)
        o_da = o_da.reshape(T, da_width)

        p = proj[:, c_rw:c_rw + rw_cols].reshape(B, S, rw_cols)
        p_prev = jnp.pad(p, ((0, 0), (1, 0), (0, 0)))[:, :-1]
        p = p + (p_prev - p) * rw_mu[l]
        r_, k_, v_, w_lo, a_lo, g_lo = jnp.split(p, [sum(rw_sizes[:n]) for n in range(1, len(rw_sizes))], axis=-1)
        w_ = -jax.nn.softplus(-(rw_w0[l] + jnp.tanh(w_lo) @ rw_w2[l])) - 0.5
        logw = -jnp.exp(w_)
        a_ = jax.nn.sigmoid(rw_a0[l] + a_lo @ rw_a2[l])
        g_ = jax.nn.sigmoid(g_lo) @ rw_g2[l]
        heads = lambda t: t.reshape(B, S, rw_heads, HEAD_DIM).transpose(0, 2, 1, 3)
        kk = heads(k_ * rw_k_k[l])
        kk = kk * lax.rsqrt(jnp.maximum(jnp.sum(kk * kk, axis=-1, keepdims=True), 1e-24))
        k_ = k_ * (1.0 + (a_ - 1.0) * rw_k_a[l])
        rh, kh, vh, ah = heads(r_), heads(k_), heads(v_), heads(a_)
        y = rwkv_scan(rh, heads(logw), kh, vh, -kk, kk * ah)
        mean = jnp.mean(y, axis=-1, keepdims=True)
        var = jnp.mean(jnp.square(y - mean), axis=-1, keepdims=True)
        y = (y - mean) * lax.rsqrt(var + RW_GN_EPS)
        y = y * rw_ln_w[l].reshape(1, rw_heads, 1, HEAD_DIM) + rw_ln_b[l].reshape(1, rw_heads, 1, HEAD_DIM)
        y = y + jnp.sum(rh * kh * rw_r_k[l][None, :, None, :], axis=-1, keepdims=True) * vh
        o_rw = (y.transpose(0, 2, 1, 3).reshape(B, S, rw_width) * g_).astype(BF16).reshape(T, rw_width)

        xbc = proj[:, c_xbc:c_xbc + ssm_xbc].reshape(B, S, ssm_xbc)
        xpad = jnp.pad(xbc, ((0, 0), (SSM_CONV - 1, 0), (0, 0)))
        conv = ssm_conv_b[l] + sum(ssm_conv_w[l][jj] * xpad[:, jj:jj + S] for jj in range(SSM_CONV))
        xbc = conv * jax.nn.sigmoid(conv)
        dt = jax.nn.softplus(proj[:, c_dt:c_dt + ssm_heads].reshape(B, S, ssm_heads) + ssm_dt_bias[l])
        o_ssm = ssd_scan(xbc, proj[:, c_z:c_z + ssm_width], dt, -jnp.exp(ssm_a_log[l]), ssm_d[l], ssm_norm_w[l],
                         B, S, ssm_heads)

        wb = w_branch[l].astype(BF16)
        merged = merge_branches(proj, o_da, o_rw, o_ssm, wb[:da_width], wb[da_width:da_width + rw_width],
                                wb[da_width + rw_width:], D)
        xt = matmul(merged, w_out[l].astype(BF16), residual=xt)

        xn2 = rmsnorm_bf16(xt, norm_ffn_w[l])
        qp = matmul(xn2, peer_w_query[l].astype(BF16))
        st, tau, c = peer_topk(qp, peer_sub_keys[l])
        xt = peer_dense(xn2, peer_down[l].astype(BF16), peer_up[l].astype(BF16), st, tau, c, xt)
    return xt.reshape(B, S, D)
```

```python
import functools
import math

import jax
import jax.numpy as jnp
from jax import lax
from jax.experimental import pallas as pl
from jax.experimental.pallas import tpu as pltpu

F32 = jnp.float32
BF16 = jnp.bfloat16
HIGHEST = lax.Precision.HIGHEST

LANES = 128
VMEM_LIMIT_BYTES = 56 * 1024 * 1024

HEAD_DIM = 64
RMS_EPS = 1e-6
ROPE_THETA = 10000.0
RW_LORA = (64, 64, 128)
RW_GN_EPS = 64e-5
RW_CHUNK = 64
SSM_GROUPS = 4
SSM_STATE = 128
SSM_CONV = 4
SSM_CHUNK = 256
SSM_NORM_EPS = 1e-5
PEER_HEADS = 8
PEER_KEYS = 128
PEER_TOPK = 16


def _cparams(*sem):
    return pltpu.CompilerParams(dimension_semantics=sem, vmem_limit_bytes=VMEM_LIMIT_BYTES)


def _tile(n, pref):
    t = min(n, pref)
    assert n % t == 0, (n, pref)
    return t


def _nt_dot(a, b, **kw):
    return lax.dot_general(a, b, (((1,), (1,)), ((), ())), preferred_element_type=F32, **kw)


def _tn_dot(a, b, **kw):
    return lax.dot_general(a, b, (((0,), (0,)), ((), ())), preferred_element_type=F32, **kw)


def _rmsnorm_kernel(x_ref, w_ref, o_ref):
    x = x_ref[...]
    ms = jnp.mean(x * x, axis=-1, keepdims=True)
    o_ref[...] = (x * lax.rsqrt(ms + RMS_EPS) * w_ref[...]).astype(o_ref.dtype)


def rmsnorm_bf16(x, w):
    T, D = x.shape
    tm = _tile(T, 256)
    return pl.pallas_call(
        _rmsnorm_kernel,
        grid=(T // tm,),
        in_specs=[pl.BlockSpec((tm, D), lambda i: (i, 0)), pl.BlockSpec((1, D), lambda i: (0, 0))],
        out_specs=pl.BlockSpec((tm, D), lambda i: (i, 0)),
        out_shape=jax.ShapeDtypeStruct((T, D), BF16),
        compiler_params=_cparams("parallel"),
    )(x, w.reshape(1, D))


def _mm_kernel(a_ref, b_ref, o_ref):
    o_ref[...] = jnp.dot(a_ref[...], b_ref[...], preferred_element_type=F32).astype(o_ref.dtype)


def _mm_res_kernel(a_ref, b_ref, r_ref, o_ref):
    o_ref[...] = r_ref[...] + jnp.dot(a_ref[...], b_ref[...], preferred_element_type=F32)


def matmul(a, b, residual=None, out_dtype=F32, tm_pref=1024, tn_pref=512):
    M, K = a.shape
    N = b.shape[1]
    tm, tn = _tile(M, tm_pref), _tile(N, tn_pref)
    in_specs = [pl.BlockSpec((tm, K), lambda i, j: (i, 0)), pl.BlockSpec((K, tn), lambda i, j: (0, j))]
    args = [a, b]
    kern = _mm_kernel
    if residual is not None:
        in_specs.append(pl.BlockSpec((tm, tn), lambda i, j: (i, j)))
        args.append(residual)
        kern = _mm_res_kernel
    return pl.pallas_call(
        kern,
        grid=(M // tm, N // tn),
        in_specs=in_specs,
        out_specs=pl.BlockSpec((tm, tn), lambda i, j: (i, j)),
        out_shape=jax.ShapeDtypeStruct((M, N), out_dtype),
        compiler_params=_cparams("parallel", "arbitrary"),
    )(*args)


def _merge_kernel(g0_ref, g1_ref, g2_ref, a0_ref, a1_ref, a2_ref, w0_ref, w1_ref, w2_ref, o_ref):
    acc = jax.nn.sigmoid(g0_ref[...]) * jnp.dot(a0_ref[...], w0_ref[...], preferred_element_type=F32)
    acc += jax.nn.sigmoid(g1_ref[...]) * jnp.dot(a1_ref[...], w1_ref[...], preferred_element_type=F32)
    acc += jax.nn.sigmoid(g2_ref[...]) * jnp.dot(a2_ref[...], w2_ref[...], preferred_element_type=F32)
    o_ref[...] = acc.astype(o_ref.dtype)


def merge_branches(proj, o_da, o_rw, o_ssm, wb_da, wb_rw, wb_ssm, D):
    T = proj.shape[0]
    tm, tn = _tile(T, 1024), _tile(D, 512)
    nb = D // tn

    def gate_spec(n):
        return pl.BlockSpec((tm, tn), lambda i, j: (i, n * nb + j))

    def act_spec(a):
        return pl.BlockSpec((tm, a.shape[1]), lambda i, j: (i, 0))

    def w_spec(w):
        return pl.BlockSpec((w.shape[0], tn), lambda i, j: (0, j))

    return pl.pallas_call(
        _merge_kernel,
        grid=(T // tm, nb),
        in_specs=[gate_spec(0), gate_spec(1), gate_spec(2), act_spec(o_da), act_spec(o_rw), act_spec(o_ssm),
                  w_spec(wb_da), w_spec(wb_rw), w_spec(wb_ssm)],
        out_specs=pl.BlockSpec((tm, tn), lambda i, j: (i, j)),
        out_shape=jax.ShapeDtypeStruct((T, D), BF16),
        compiler_params=_cparams("parallel", "arbitrary"),
    )(proj, proj, proj, o_da, o_rw, o_ssm, wb_da, wb_rw, wb_ssm)


def _diff_attn_kernel(lam_ref, qt_ref, k_ref, vt_ref, nw_ref, o_ref, m_ref, l_ref, acc_ref, *, tq, hp, out_scale):
    qi = pl.program_id(2)
    qmaps = []
    for h in range(hp):
        qt = qt_ref[0, h, 0]
        dim = lax.broadcasted_iota(jnp.int32, qt.shape, 0)
        zero = jnp.zeros_like(qt)
        qmaps += [jnp.where(dim < HEAD_DIM, qt, zero), jnp.where(dim >= HEAD_DIM, qt, zero)]
    m_ref[...] = jnp.full(m_ref.shape, -jnp.inf, F32)
    l_ref[...] = jnp.zeros(l_ref.shape, F32)
    acc_ref[...] = jnp.zeros(acc_ref.shape, F32)

    def step(j, masked):
        kt = k_ref[0, pl.ds(j * tq, tq), :]
        scores = [jnp.dot(kt[:, (c // 2) * LANES:(c // 2 + 1) * LANES], qc, preferred_element_type=F32)
                  for c, qc in enumerate(qmaps)]
        for c, s in enumerate(scores):
            vt = vt_ref[0, c // 2, j]
            if masked:
                key = lax.broadcasted_iota(jnp.int32, (tq, tq), 0)
                qry = lax.broadcasted_iota(jnp.int32, (tq, tq), 1)
                s = jnp.where(key <= qry, s, -jnp.inf)
            m_old = m_ref[c]
            m_new = jnp.maximum(m_old, jnp.max(s, axis=0, keepdims=True))
            alpha = jnp.exp(m_old - m_new)
            p = jnp.exp(s - m_new)
            l_ref[c] = alpha * l_ref[c] + jnp.sum(p, axis=0, keepdims=True)
            acc_ref[c] = alpha * acc_ref[c] + jnp.dot(vt, p.astype(vt.dtype), preferred_element_type=F32)
            m_ref[c] = m_new

    def body(j, c):
        step(j, False)
        return c

    lax.fori_loop(0, qi, body, 0)
    step(qi, True)
    for h in range(hp):
        o = acc_ref[2 * h] / l_ref[2 * h] - lam_ref[0] * (acc_ref[2 * h + 1] / l_ref[2 * h + 1])
        ms = jnp.mean(o * o, axis=0, keepdims=True)
        o = o * lax.rsqrt(ms + RMS_EPS) * nw_ref[...] * out_scale
        o_ref[0, :, h * LANES:(h + 1) * LANES] = o.T.astype(o_ref.dtype)


def diff_attention(lam, q, k, v, sub_norm_w, lambda_init):
    B, S, W = q.shape
    H = W // LANES
    tq = _tile(S, 512)
    nq = S // tq
    hp = _tile(H, 2)
    to_t = lambda t: t.reshape(B, nq, tq, H, LANES).transpose(0, 3, 1, 4, 2)
    kern = functools.partial(_diff_attn_kernel, tq=tq, hp=hp, out_scale=1.0 - lambda_init)
    return pl.pallas_call(
        kern,
        grid=(B, H // hp, nq),
        in_specs=[pl.BlockSpec(memory_space=pltpu.SMEM),
                  pl.BlockSpec((1, hp, 1, LANES, tq), lambda b, h, i: (b, h, i, 0, 0)),
                  pl.BlockSpec((1, S, hp * LANES), lambda b, h, i: (b, 0, h)),
                  pl.BlockSpec((1, hp, nq, LANES, tq), lambda b, h, i: (b, h, 0, 0, 0)),
                  pl.BlockSpec((LANES, 1), lambda b, h, i: (0, 0))],
        out_specs=pl.BlockSpec((1, tq, hp * LANES), lambda b, h, i: (b, i, h)),
        out_shape=jax.ShapeDtypeStruct((B, S, W), BF16),
        scratch_shapes=[pltpu.VMEM((2 * hp, 1, tq), F32), pltpu.VMEM((2 * hp, 1, tq), F32),
                        pltpu.VMEM((2 * hp, LANES, tq), F32)],
        compiler_params=_cparams("parallel", "parallel", "arbitrary"),
    )(lam, to_t(q), k, to_t(v), sub_norm_w.reshape(LANES, 1))


def _rotary(x, cos, sin):
    x1, x2 = jnp.split(x, 2, axis=-1)
    c, s = cos[None, :, None, :], sin[None, :, None, :]
    return jnp.concatenate([x1 * c - x2 * s, x2 * c + x1 * s], axis=-1)


def _head_rms(x, w):
    return x * lax.rsqrt(jnp.mean(x * x, axis=-1, keepdims=True) + RMS_EPS) * w


def _rwkv_chunk_kernel(r_ref, lw_ref, k_ref, v_ref, a_ref, b_ref, y_ref, state_ref, *, hb, C):
    @pl.when(pl.program_id(2) == 0)
    def _():
        state_ref[...] = jnp.zeros(state_ref.shape, F32)

    ti = lax.broadcasted_iota(jnp.int32, (C, C), 0)
    si = lax.broadcasted_iota(jnp.int32, (C, C), 1)
    incl = si <= ti
    strict = si < ti
    tri = incl.astype(F32)
    bf = lambda t: t.astype(BF16)
    dot = lambda x, y: jnp.dot(bf(x), bf(y), preferred_element_type=F32)
    hs = range(hb)
    r, lw, k, v, a, b = ([ref[0, h] for h in hs] for ref in (r_ref, lw_ref, k_ref, v_ref, a_ref, b_ref))
    cum = [jnp.dot(tri, lw[h], preferred_element_type=F32, precision=HIGHEST) for h in hs]
    ar = [bf(jnp.concatenate([a[h] * jnp.exp(cum[h] - lw[h]), r[h] * jnp.exp(cum[h])], axis=0)) for h in hs]
    inv = [jnp.exp(-cum[h]) for h in hs]
    bk = [bf(jnp.concatenate([b[h] * inv[h], k[h] * inv[h]], axis=0)) for h in hs]
    g = [_nt_dot(ar[h], bk[h]) for h in hs]
    s0 = [state_ref[h] for h in hs]
    x = [_nt_dot(ar[h], bf(s0[h])) for h in hs]
    u = [x[h][:C] + dot(jnp.where(strict, g[h][:C, C:], 0.0), v[h]) for h in hs]
    lp = [jnp.where(strict, g[h][:C, :C], 0.0) for h in hs]
    n_double = max(1, (C - 1).bit_length())
    for it in range(n_double):
        u = [u[h] + dot(lp[h], u[h]) for h in hs]
        if it + 1 < n_double:
            lp = [dot(lp[h], lp[h]) for h in hs]
    for h in hs:
        m_rb = jnp.where(incl, g[h][C:, :C], 0.0)
        m_rk = jnp.where(incl, g[h][C:, C:], 0.0)
        y_ref[0, h] = x[h][C:] + dot(m_rb, u[h]) + dot(m_rk, v[h])
    for h in hs:
        to_end = jnp.exp(cum[h][C - 1:C, :] - cum[h])
        uv = jnp.concatenate([u[h], v[h]], axis=0)
        bk_end = jnp.concatenate([b[h] * to_end, k[h] * to_end], axis=0)
        state_ref[h] = s0[h] * jnp.exp(cum[h][C - 1:C, :]) + _tn_dot(bf(uv), bf(bk_end))


def rwkv_scan(r, lw, k, v, a, b):
    B, H, S, N = r.shape
    C = _tile(S, RW_CHUNK)
    hb = _tile(H, 8)
    spec = pl.BlockSpec((1, hb, C, N), lambda bi, hi, ci: (bi, hi, ci, 0))
    return pl.pallas_call(
        functools.partial(_rwkv_chunk_kernel, hb=hb, C=C),
        grid=(B, H // hb, S // C),
        in_specs=[spec] * 6,
        out_specs=spec,
        out_shape=jax.ShapeDtypeStruct((B, H, S, N), F32),
        scratch_shapes=[pltpu.VMEM((hb, N, N), F32)],
        compiler_params=_cparams("parallel", "parallel", "arbitrary"),
    )(r, lw, k, v, a, b)


def _conv_silu(x, tail_ref, w_ref, b_ref):
    L = x.shape[0]
    row = lax.broadcasted_iota(jnp.int32, x.shape, 0)
    tail = tail_ref[...]
    w = w_ref[...]
    acc = b_ref[...] + w[SSM_CONV - 1:SSM_CONV, :] * x
    for s in range(1, SSM_CONV):
        cur = pltpu.roll(x, s, axis=0)
        prev = pltpu.roll(tail, s, axis=0)
        shifted = jnp.where(row < s, jnp.concatenate([prev, cur[8:]], axis=0), cur)
        acc += w[SSM_CONV - 1 - s:SSM_CONV - s, :] * shifted
    tail_ref[...] = x[L - 8:, :]
    return acc * jax.nn.sigmoid(acc)


def _ssd_kernel(xs_ref, bm_ref, cm_ref, z_ref, dt_ref, dtt_ref, a_ref, acol_ref, d_ref, nw_ref, ex_ref,
                wx_ref, wb_ref, wc_ref, bx_ref, bb_ref, bc_ref, o_ref,
                state_ref, tx_ref, tb_ref, tc_ref, *, L, E, P):
    @pl.when(pl.program_id(2) == 0)
    def _():
        state_ref[...] = jnp.zeros(state_ref.shape, F32)
        tx_ref[...] = jnp.zeros(tx_ref.shape, F32)
        tb_ref[...] = jnp.zeros(tb_ref.shape, F32)
        tc_ref[...] = jnp.zeros(tc_ref.shape, F32)

    li = lax.broadcasted_iota(jnp.int32, (L, L), 0)
    si = lax.broadcasted_iota(jnp.int32, (L, L), 1)
    lower = si <= li
    hdot = functools.partial(jnp.dot, preferred_element_type=F32, precision=HIGHEST)
    ex = ex_ref[...]
    a_row = a_ref[0]
    dt = dt_ref[0, 0]
    cs = hdot(lower.astype(F32), dt * a_row)
    cs_row = hdot(dtt_ref[0, 0] * acol_ref[0], (li <= si).astype(F32))
    cs_last = cs[L - 1:L, :]
    xs = _conv_silu(xs_ref[...], tx_ref, wx_ref, bx_ref)
    xdt = xs * hdot(dt, ex)
    bm = _conv_silu(bm_ref[...], tb_ref, wb_ref, bb_ref).astype(BF16)
    cm = _conv_silu(cm_ref[...], tc_ref, wc_ref, bc_ref).astype(BF16)
    cb = _nt_dot(cm, bm)
    state_in = state_ref[...]
    y = hdot(jnp.exp(cs), ex) * jnp.dot(cm, state_in.astype(BF16), preferred_element_type=F32)
    y += d_ref[0] * xs
    lane = lax.broadcasted_iota(jnp.int32, (L, 2 * P), 1)
    ydiag = []
    for pr in range(E // 2):
        xpair = xdt[:, pr * 2 * P:(pr + 1) * 2 * P]
        acc = jnp.zeros((L, 2 * P), F32)
        for sub in range(2):
            e = 2 * pr + sub
            seg = jnp.where(lower, jnp.exp(jnp.minimum(cs[:, e:e + 1] - cs_row[e:e + 1, :], 0.0)), 0.0)
            xm = jnp.where((lane >= sub * P) & (lane < (sub + 1) * P), xpair, 0.0)
            acc += jnp.dot((cb * seg).astype(BF16), xm.astype(BF16), preferred_element_type=F32)
        ydiag.append(acc)
    y += jnp.concatenate(ydiag, axis=1)
    xend = (xdt * hdot(jnp.exp(cs_last - cs), ex)).astype(BF16)
    state_ref[...] = hdot(jnp.exp(cs_last), ex) * state_in + _tn_dot(bm, xend)
    zg = z_ref[...]
    y = y * (zg * jax.nn.sigmoid(zg))
    y = y * lax.rsqrt(jnp.mean(y * y, axis=-1, keepdims=True) + SSM_NORM_EPS)
    o_ref[...] = (y * nw_ref[...]).astype(o_ref.dtype)


def ssd_scan(proj, col_z, col_xbc, conv_w, conv_b, dt, a_neg, d_skip, norm_w, B, S, H):
    G, N, P = SSM_GROUPS, SSM_STATE, HEAD_DIM
    E = H // G
    EP = E * P
    W = H * P
    L = math.gcd(S, SSM_CHUNK)
    nc = S // L
    assert col_z % EP == 0 and col_xbc % EP == 0 and L % 8 == 0
    zb, xb, bb = col_z // EP, col_xbc // EP, (col_xbc + W) // N
    conv_b = conv_b.reshape(1, -1)
    dt_g = dt.reshape(B, S, G, E).transpose(0, 2, 1, 3)
    dt_gt = dt_g.transpose(0, 1, 3, 2)
    ex = (jnp.arange(EP)[None, :] // P == jnp.arange(E)[:, None]).astype(F32)
    d_full = jnp.repeat(d_skip.reshape(G, 1, E), P, axis=2)
    kern = functools.partial(_ssd_kernel, L=L, E=E, P=P)
    return pl.pallas_call(
        kern,
        grid=(B, G, nc),
        in_specs=[pl.BlockSpec((L, EP), lambda b, g, c: (b * nc + c, xb + g)),
                  pl.BlockSpec((L, N), lambda b, g, c: (b * nc + c, bb + g)),
                  pl.BlockSpec((L, N), lambda b, g, c: (b * nc + c, bb + G + g)),
                  pl.BlockSpec((L, EP), lambda b, g, c: (b * nc + c, zb + g)),
                  pl.BlockSpec((1, 1, L, E), lambda b, g, c: (b, g, c, 0)),
                  pl.BlockSpec((1, 1, E, L), lambda b, g, c: (b, g, 0, c)),
                  pl.BlockSpec((1, 1, E), lambda b, g, c: (g, 0, 0)),
                  pl.BlockSpec((1, E, 1), lambda b, g, c: (g, 0, 0)),
                  pl.BlockSpec((1, 1, EP), lambda b, g, c: (g, 0, 0)),
                  pl.BlockSpec((1, EP), lambda b, g, c: (0, g)),
                  pl.BlockSpec((E, EP), lambda b, g, c: (0, 0)),
                  pl.BlockSpec((SSM_CONV, EP), lambda b, g, c: (0, g)),
                  pl.BlockSpec((SSM_CONV, N), lambda b, g, c: (0, W // N + g)),
                  pl.BlockSpec((SSM_CONV, N), lambda b, g, c: (0, W // N + G + g)),
                  pl.BlockSpec((1, EP), lambda b, g, c: (0, g)),
                  pl.BlockSpec((1, N), lambda b, g, c: (0, W // N + g)),
                  pl.BlockSpec((1, N), lambda b, g, c: (0, W // N + G + g))],
        out_specs=pl.BlockSpec((L, EP), lambda b, g, c: (b * nc + c, g)),
        out_shape=jax.ShapeDtypeStruct((B * S, W), BF16),
        scratch_shapes=[pltpu.VMEM((N, EP), F32), pltpu.VMEM((8, EP), F32), pltpu.VMEM((8, N), F32),
                        pltpu.VMEM((8, N), F32)],
        compiler_params=_cparams("parallel", "parallel", "arbitrary"),
    )(proj, proj, proj, proj, dt_g, dt_gt, a_neg.reshape(G, 1, E), a_neg.reshape(G, E, 1), d_full,
      norm_w.reshape(1, W), ex, conv_w, conv_w, conv_w, conv_b, conv_b, conv_b)


def _extract_top(v, n):
    rows = []
    for _ in range(n):
        m = jnp.max(v, axis=0, keepdims=True)
        rows.append(m)
        v = jnp.where(v == m, -jnp.inf, v)
    return rows


def _peer_topk_kernel(q_ref, keys_ref, st_ref, tau_ref, c_ref, *, half):
    tops = []
    for hp in range(2 * PEER_HEADS):
        st = _nt_dot(keys_ref[hp], q_ref[:, hp * half:(hp + 1) * half].astype(BF16))
        st_ref[hp * PEER_KEYS:(hp + 1) * PEER_KEYS, :] = st
        tops.append(_extract_top(st, PEER_TOPK))
    for hd in range(PEER_HEADS):
        a_rows, b_rows = tops[2 * hd], tops[2 * hd + 1]
        cand = [a_rows[i] + b_rows[j] for i in range(PEER_TOPK) for j in range(PEER_TOPK // (i + 1))]
        pad = (-len(cand)) % 8
        cand = jnp.concatenate(cand + [jnp.full_like(cand[0], -jnp.inf)] * pad, axis=0)
        best = _extract_top(cand, PEER_TOPK)
        tau_ref[hd:hd + 1, :] = best[PEER_TOPK - 1]
        tot = jnp.exp(best[0] - best[0])
        for bi in best[1:]:
            tot += jnp.exp(bi - best[0])
        c_ref[hd:hd + 1, :] = best[0] + jnp.log(tot)


def peer_topk(qp, sub_keys):
    T = qp.shape[0]
    half = sub_keys.shape[-1]
    tm = _tile(T, 512)
    rows = 2 * PEER_HEADS * PEER_KEYS
    keys = sub_keys.reshape(2 * PEER_HEADS, PEER_KEYS, half).astype(BF16)
    return pl.pallas_call(
        functools.partial(_peer_topk_kernel, half=half),
        grid=(T // tm,),
        in_specs=[pl.BlockSpec((tm, qp.shape[1]), lambda i: (i, 0)),
                  pl.BlockSpec(keys.shape, lambda i: (0, 0, 0))],
        out_specs=[pl.BlockSpec((rows, tm), lambda i: (0, i)),
                   pl.BlockSpec((PEER_HEADS, tm), lambda i: (0, i)),
                   pl.BlockSpec((PEER_HEADS, tm), lambda i: (0, i))],
        out_shape=[jax.ShapeDtypeStruct((rows, T), F32), jax.ShapeDtypeStruct((PEER_HEADS, T), F32),
                   jax.ShapeDtypeStruct((PEER_HEADS, T), F32)],
        compiler_params=_cparams("parallel"),
    )(qp, keys)


def _peer_kernel(xn_ref, down_ref, up_ref, st_ref, tau_ref, c_ref, res_ref, o_ref, w_ref, hid_ref, *, tm, tn, nj):
    j = pl.program_id(1)
    na = tn // PEER_KEYS

    @pl.when(j == 0)
    def _():
        o_ref[...] = res_ref[...]
        w_ref[1] = jnp.zeros(w_ref.shape[1:], w_ref.dtype)

    jt = jnp.minimum(j, nj - 1)

    def run(rd, wr):
        s0rows = [[st_ref[pl.ds(2 * hd * PEER_KEYS + jt * na + ai, 1), :] for hd in range(PEER_HEADS)]
                  for ai in range(na)]
        o_ref[...] += jnp.dot(w_ref[rd], up_ref[...], preferred_element_type=F32)
        hid_ref[...] = jnp.dot(xn_ref[...], down_ref[...], preferred_element_type=F32)
        for tb in range(tm // LANES):
            tok = slice(tb * LANES, (tb + 1) * LANES)
            for ai in range(na):
                exp_cols = slice(ai * PEER_KEYS, (ai + 1) * PEER_KEYS)
                g = jnp.zeros((PEER_KEYS, LANES), F32)
                for hd in range(PEER_HEADS):
                    s1 = st_ref[(2 * hd + 1) * PEER_KEYS:(2 * hd + 2) * PEER_KEYS, tok]
                    z = s0rows[ai][hd][:, tok] + s1
                    g += jnp.where(z >= tau_ref[hd:hd + 1, tok], jnp.exp(z - c_ref[hd:hd + 1, tok]), 0.0)
                hid = hid_ref[tok, exp_cols]
                act = 0.5 * hid * (1.0 + lax.erf(hid * (2.0 ** -0.5)))
                w_ref[wr, tok, exp_cols] = (g.T * act).astype(w_ref.dtype)

    @pl.when(j % 2 == 0)
    def _():
        run(1, 0)

    @pl.when(j % 2 == 1)
    def _():
        run(0, 1)


def peer_dense(xn, down_t, up, st, tau, c, res):
    T, D = xn.shape
    NE = up.shape[0]
    tm, tn = _tile(T, 512), _tile(NE, 256)
    nj = NE // tn
    once = pl.Buffered(1)
    return pl.pallas_call(
        functools.partial(_peer_kernel, tm=tm, tn=tn, nj=nj),
        grid=(T // tm, nj + 1),
        in_specs=[pl.BlockSpec((tm, D), lambda i, j: (i, 0), pipeline_mode=once),
                  pl.BlockSpec((D, tn), lambda i, j: (0, jnp.minimum(j, nj - 1))),
                  pl.BlockSpec((tn, D), lambda i, j: (jnp.maximum(j - 1, 0), 0)),
                  pl.BlockSpec((st.shape[0], tm), lambda i, j: (0, i), pipeline_mode=once),
                  pl.BlockSpec((PEER_HEADS, tm), lambda i, j: (0, i)),
                  pl.BlockSpec((PEER_HEADS, tm), lambda i, j: (0, i)),
                  pl.BlockSpec((tm, D), lambda i, j: (i, 0), pipeline_mode=once)],
        out_specs=pl.BlockSpec((tm, D), lambda i, j: (i, 0)),
        out_shape=jax.ShapeDtypeStruct((T, D), F32),
        scratch_shapes=[pltpu.VMEM((2, tm, tn), BF16), pltpu.VMEM((tm, tn), F32)],
        compiler_params=_cparams("parallel", "arbitrary"),
    )(xn, down_t, up, st, tau, c, res)


PROJ_ALIGN = 512


def _proj_weight(w, sizes, offs, total):
    parts, src = [], 0
    for sz, off, nxt in zip(sizes, offs, offs[1:] + [total]):
        seg = w[:, src:src + sz]
        src += sz
        parts.append(jnp.pad(seg, ((0, 0), (0, nxt - off - sz))))
    return jnp.concatenate(parts, axis=1).astype(BF16)


def kernel(x, norm_mix_w, w_in, da_q_norm_w, da_k_norm_w, da_lambda, da_sub_norm_w, rw_mu, rw_w0, rw_w2, rw_a0, rw_a2, rw_g2, rw_k_k, rw_k_a, rw_r_k, rw_ln_w, rw_ln_b, ssm_conv_w, ssm_conv_b, ssm_dt_bias, ssm_a_log, ssm_d, ssm_norm_w, w_branch, w_out, norm_ffn_w, peer_w_query, peer_sub_keys, peer_down, peer_up):
    B, S, D = x.shape
    T = B * S
    depth = w_in.shape[0]
    da_heads = D // 512
    da_cols = 2 * da_heads * HEAD_DIM
    da_width = da_heads * 2 * HEAD_DIM
    rw_heads = D // 256
    rw_width = rw_heads * HEAD_DIM
    rw_sizes = (rw_width, rw_width, rw_width) + RW_LORA
    rw_cols = sum(rw_sizes)
    ssm_heads = D // 128
    ssm_width = ssm_heads * HEAD_DIM
    ssm_xbc = ssm_width + 2 * SSM_GROUPS * SSM_STATE
    sizes = (3 * D, da_cols, da_cols, da_width, rw_cols, ssm_width, ssm_xbc, ssm_heads)
    assert sum(sizes) == w_in.shape[2]
    offs, proj_cols = [], 0
    for sz in sizes:
        offs.append(proj_cols)
        proj_cols += -(-sz // PROJ_ALIGN) * PROJ_ALIGN
    c_q, c_k, c_v, c_rw, c_z, c_xbc, c_dt = offs[1:8]

    pos = jnp.arange(S, dtype=F32)
    inv_freq = ROPE_THETA ** (-jnp.arange(0, HEAD_DIM, 2, dtype=F32) / HEAD_DIM)
    ang = pos[:, None] * inv_freq[None, :]
    cos, sin = jnp.cos(ang), jnp.sin(ang)

    xt = x.reshape(T, D)
    for l in range(depth):
        lambda_init = 0.8 - 0.6 * math.exp(-0.3 * l)
        xn = rmsnorm_bf16(xt, norm_mix_w[l])
        proj = matmul(xn, _proj_weight(w_in[l], sizes, offs, proj_cols))

        q = proj[:, c_q:c_q + da_cols].reshape(B, S, 2 * da_heads, HEAD_DIM)
        k = proj[:, c_k:c_k + da_cols].reshape(B, S, 2 * da_heads, HEAD_DIM)
        q = _rotary(_head_rms(q, da_q_norm_w[l]), cos, sin) * (HEAD_DIM ** -0.5)
        k = _rotary(_head_rms(k, da_k_norm_w[l]), cos, sin)
        v = proj[:, c_v:c_v + da_width].reshape(B, S, da_width)
        lv = da_lambda[l]
        lam = jnp.exp(jnp.sum(lv[0] * lv[1])) - jnp.exp(jnp.sum(lv[2] * lv[3])) + lambda_init
        o_da = diff_attention(lam.reshape(1), q.reshape(B, S, da_cols).astype(BF16),
                              k.reshape(B, S, da_cols).astype(BF16), v.astype(BF16), da_sub_norm_w[l], lambda_init)
        o_da = o_da.reshape(T, da_width)

        p = proj[:, c_rw:c_rw + rw_cols].reshape(B, S, rw_cols)
        p_prev = jnp.pad(p, ((0, 0), (1, 0), (0, 0)))[:, :-1]
        p = p + (p_prev - p) * rw_mu[l]
        r_, k_, v_, w_lo, a_lo, g_lo = jnp.split(p, [sum(rw_sizes[:n]) for n in range(1, len(rw_sizes))], axis=-1)
        w_ = -jax.nn.softplus(-(rw_w0[l] + jnp.tanh(w_lo) @ rw_w2[l])) - 0.5
        logw = -jnp.exp(w_)
        a_ = jax.nn.sigmoid(rw_a0[l] + a_lo @ rw_a2[l])
        g_ = jax.nn.sigmoid(g_lo) @ rw_g2[l]
        heads = lambda t: t.reshape(B, S, rw_heads, HEAD_DIM).transpose(0, 2, 1, 3)
        kk = heads(k_ * rw_k_k[l])
        kk = kk * lax.rsqrt(jnp.maximum(jnp.sum(kk * kk, axis=-1, keepdims=True), 1e-24))
        k_ = k_ * (1.0 + (a_ - 1.0) * rw_k_a[l])
        rh, kh, vh, ah = heads(r_), heads(k_), heads(v_), heads(a_)
        y = rwkv_scan(rh, heads(logw), kh, vh, -kk, kk * ah)
        mean = jnp.mean(y, axis=-1, keepdims=True)
        var = jnp.mean(jnp.square(y - mean), axis=-1, keepdims=True)
        y = (y - mean) * lax.rsqrt(var + RW_GN_EPS)
        y = y * rw_ln_w[l].reshape(1, rw_heads, 1, HEAD_DIM) + rw_ln_b[l].reshape(1, rw_heads, 1, HEAD_DIM)
        y = y + jnp.sum(rh * kh * rw_r_k[l][None, :, None, :], axis=-1, keepdims=True) * vh
        o_rw = (y.transpose(0, 2, 1, 3).reshape(B, S, rw_width) * g_).astype(BF16).reshape(T, rw_width)

        dt = jax.nn.softplus(proj[:, c_dt:c_dt + ssm_heads].reshape(B, S, ssm_heads) + ssm_dt_bias[l])
        o_ssm = ssd_scan(proj, c_z, c_xbc, ssm_conv_w[l], ssm_conv_b[l], dt, -jnp.exp(ssm_a_log[l]), ssm_d[l],
                         ssm_norm_w[l], B, S, ssm_heads)

        wb = w_branch[l].astype(BF16)
        merged = merge_branches(proj, o_da, o_rw, o_ssm, wb[:da_width], wb[da_width:da_width + rw_width],
                                wb[da_width + rw_width:], D)
        xt = matmul(merged, w_out[l].astype(BF16), residual=xt)

        xn2 = rmsnorm_bf16(xt, norm_ffn_w[l])
        qp = matmul(xn2, peer_w_query[l].astype(BF16))
        st, tau, c = peer_topk(qp, peer_sub_keys[l])
        xt = peer_dense(xn2, peer_down[l].astype(BF16).T, peer_up[l].astype(BF16), st, tau, c, xt)
    return xt.reshape(B, S, D)
```

```python
import functools
import math

import jax
import jax.numpy as jnp
from jax import lax
from jax.experimental import pallas as pl
from jax.experimental.pallas import tpu as pltpu

F32 = jnp.float32
BF16 = jnp.bfloat16
HIGHEST = lax.Precision.HIGHEST

LANES = 128
VMEM_LIMIT_BYTES = 56 * 1024 * 1024

HEAD_DIM = 64
RMS_EPS = 1e-6
ROPE_THETA = 10000.0
RW_LORA = (64, 64, 128)
RW_GN_EPS = 64e-5
RW_CHUNK = 64
SSM_GROUPS = 4
SSM_STATE = 128
SSM_CONV = 4
SSM_CHUNK = 256
SSM_NORM_EPS = 1e-5
PEER_HEADS = 8
PEER_KEYS = 128
PEER_TOPK = 16


def _cparams(*sem):
    return pltpu.CompilerParams(dimension_semantics=sem, vmem_limit_bytes=VMEM_LIMIT_BYTES)


def _tile(n, pref):
    t = min(n, pref)
    assert n % t == 0, (n, pref)
    return t


def _nt_dot(a, b, **kw):
    return lax.dot_general(a, b, (((1,), (1,)), ((), ())), preferred_element_type=F32, **kw)


def _tn_dot(a, b, **kw):
    return lax.dot_general(a, b, (((0,), (0,)), ((), ())), preferred_element_type=F32, **kw)


def _rmsnorm_kernel(x_ref, w_ref, o_ref):
    x = x_ref[...]
    ms = jnp.mean(x * x, axis=-1, keepdims=True)
    o_ref[...] = (x * lax.rsqrt(ms + RMS_EPS) * w_ref[...]).astype(o_ref.dtype)


def rmsnorm_bf16(x, w):
    T, D = x.shape
    tm = _tile(T, 256)
    return pl.pallas_call(
        _rmsnorm_kernel,
        grid=(T // tm,),
        in_specs=[pl.BlockSpec((tm, D), lambda i: (i, 0)), pl.BlockSpec((1, D), lambda i: (0, 0))],
        out_specs=pl.BlockSpec((tm, D), lambda i: (i, 0)),
        out_shape=jax.ShapeDtypeStruct((T, D), BF16),
        compiler_params=_cparams("parallel"),
    )(x, w.reshape(1, D))


def _mm_kernel(a_ref, b_ref, o_ref):
    o_ref[...] = jnp.dot(a_ref[...], b_ref[...], preferred_element_type=F32).astype(o_ref.dtype)


def _mm_res_kernel(a_ref, b_ref, r_ref, o_ref):
    o_ref[...] = r_ref[...] + jnp.dot(a_ref[...], b_ref[...], preferred_element_type=F32)


def matmul(a, b, residual=None, out_dtype=F32, tm_pref=1024, tn_pref=512):
    M, K = a.shape
    N = b.shape[1]
    tm, tn = _tile(M, tm_pref), _tile(N, tn_pref)
    in_specs = [pl.BlockSpec((tm, K), lambda i, j: (i, 0)), pl.BlockSpec((K, tn), lambda i, j: (0, j))]
    args = [a, b]
    kern = _mm_kernel
    if residual is not None:
        in_specs.append(pl.BlockSpec((tm, tn), lambda i, j: (i, j)))
        args.append(residual)
        kern = _mm_res_kernel
    return pl.pallas_call(
        kern,
        grid=(M // tm, N // tn),
        in_specs=in_specs,
        out_specs=pl.BlockSpec((tm, tn), lambda i, j: (i, j)),
        out_shape=jax.ShapeDtypeStruct((M, N), out_dtype),
        compiler_params=_cparams("parallel", "arbitrary"),
    )(*args)


def _merge_kernel(g0_ref, g1_ref, g2_ref, a0_ref, a1_ref, a2_ref, w0_ref, w1_ref, w2_ref, o_ref):
    acc = jax.nn.sigmoid(g0_ref[...]) * jnp.dot(a0_ref[...], w0_ref[...], preferred_element_type=F32)
    acc += jax.nn.sigmoid(g1_ref[...]) * jnp.dot(a1_ref[...], w1_ref[...], preferred_element_type=F32)
    acc += jax.nn.sigmoid(g2_ref[...]) * jnp.dot(a2_ref[...], w2_ref[...], preferred_element_type=F32)
    o_ref[...] = acc.astype(o_ref.dtype)


def merge_branches(proj, o_da, o_rw, o_ssm, wb_da, wb_rw, wb_ssm, D):
    T = proj.shape[0]
    tm, tn = _tile(T, 1024), _tile(D, 512)
    nb = D // tn

    def gate_spec(n):
        return pl.BlockSpec((tm, tn), lambda i, j: (i, n * nb + j))

    def act_spec(a):
        return pl.BlockSpec((tm, a.shape[1]), lambda i, j: (i, 0))

    def w_spec(w):
        return pl.BlockSpec((w.shape[0], tn), lambda i, j: (0, j))

    return pl.pallas_call(
        _merge_kernel,
        grid=(T // tm, nb),
        in_specs=[gate_spec(0), gate_spec(1), gate_spec(2), act_spec(o_da), act_spec(o_rw), act_spec(o_ssm),
                  w_spec(wb_da), w_spec(wb_rw), w_spec(wb_ssm)],
        out_specs=pl.BlockSpec((tm, tn), lambda i, j: (i, j)),
        out_shape=jax.ShapeDtypeStruct((T, D), BF16),
        compiler_params=_cparams("parallel", "arbitrary"),
    )(proj, proj, proj, o_da, o_rw, o_ssm, wb_da, wb_rw, wb_ssm)


def _diff_attn_kernel(lam_ref, qt_ref, k_ref, vt_ref, nw_ref, o_ref, m_ref, l_ref, acc_ref, *, tq, hp, out_scale):
    qi = pl.program_id(2)
    qmaps = []
    for h in range(hp):
        qt = qt_ref[0, h, 0]
        dim = lax.broadcasted_iota(jnp.int32, qt.shape, 0)
        zero = jnp.zeros_like(qt)
        qmaps += [jnp.where(dim < HEAD_DIM, qt, zero), jnp.where(dim >= HEAD_DIM, qt, zero)]
    m_ref[...] = jnp.full(m_ref.shape, -jnp.inf, F32)
    l_ref[...] = jnp.zeros(l_ref.shape, F32)
    acc_ref[...] = jnp.zeros(acc_ref.shape, F32)

    def step(j, masked):
        kt = k_ref[0, pl.ds(j * tq, tq), :]
        scores = [jnp.dot(kt[:, (c // 2) * LANES:(c // 2 + 1) * LANES], qc, preferred_element_type=F32)
                  for c, qc in enumerate(qmaps)]
        for c, s in enumerate(scores):
            vt = vt_ref[0, c // 2, j]
            if masked:
                key = lax.broadcasted_iota(jnp.int32, (tq, tq), 0)
                qry = lax.broadcasted_iota(jnp.int32, (tq, tq), 1)
                s = jnp.where(key <= qry, s, -jnp.inf)
            m_old = m_ref[c]
            m_new = jnp.maximum(m_old, jnp.max(s, axis=0, keepdims=True))
            alpha = jnp.exp(m_old - m_new)
            p = jnp.exp(s - m_new)
            l_ref[c] = alpha * l_ref[c] + jnp.sum(p, axis=0, keepdims=True)
            acc_ref[c] = alpha * acc_ref[c] + jnp.dot(vt, p.astype(vt.dtype), preferred_element_type=F32)
            m_ref[c] = m_new

    def body(j, c):
        step(j, False)
        return c

    lax.fori_loop(0, qi, body, 0)
    step(qi, True)
    for h in range(hp):
        o = acc_ref[2 * h] / l_ref[2 * h] - lam_ref[0] * (acc_ref[2 * h + 1] / l_ref[2 * h + 1])
        ms = jnp.mean(o * o, axis=0, keepdims=True)
        o = o * lax.rsqrt(ms + RMS_EPS) * nw_ref[...] * out_scale
        o_ref[0, :, h * LANES:(h + 1) * LANES] = o.T.astype(o_ref.dtype)


def diff_attention(lam, q, k, v, sub_norm_w, lambda_init):
    B, S, W = q.shape
    H = W // LANES
    tq = _tile(S, 512)
    nq = S // tq
    hp = _tile(H, 2)
    to_t = lambda t: t.reshape(B, nq, tq, H, LANES).transpose(0, 3, 1, 4, 2)
    kern = functools.partial(_diff_attn_kernel, tq=tq, hp=hp, out_scale=1.0 - lambda_init)
    return pl.pallas_call(
        kern,
        grid=(B, H // hp, nq),
        in_specs=[pl.BlockSpec(memory_space=pltpu.SMEM),
                  pl.BlockSpec((1, hp, 1, LANES, tq), lambda b, h, i: (b, h, i, 0, 0)),
                  pl.BlockSpec((1, S, hp * LANES), lambda b, h, i: (b, 0, h)),
                  pl.BlockSpec((1, hp, nq, LANES, tq), lambda b, h, i: (b, h, 0, 0, 0)),
                  pl.BlockSpec((LANES, 1), lambda b, h, i: (0, 0))],
        out_specs=pl.BlockSpec((1, tq, hp * LANES), lambda b, h, i: (b, i, h)),
        out_shape=jax.ShapeDtypeStruct((B, S, W), BF16),
        scratch_shapes=[pltpu.VMEM((2 * hp, 1, tq), F32), pltpu.VMEM((2 * hp, 1, tq), F32),
                        pltpu.VMEM((2 * hp, LANES, tq), F32)],
        compiler_params=_cparams("parallel", "parallel", "arbitrary"),
    )(lam, to_t(q), k, to_t(v), sub_norm_w.reshape(LANES, 1))


def _rotary(x, cos, sin):
    x1, x2 = jnp.split(x, 2, axis=-1)
    c, s = cos[None, :, None, :], sin[None, :, None, :]
    return jnp.concatenate([x1 * c - x2 * s, x2 * c + x1 * s], axis=-1)


def _head_rms(x, w):
    return x * lax.rsqrt(jnp.mean(x * x, axis=-1, keepdims=True) + RMS_EPS) * w


def _rwkv_chunk_kernel(r_ref, lw_ref, k_ref, v_ref, a_ref, b_ref, y_ref, state_ref, *, hb, C):
    @pl.when(pl.program_id(2) == 0)
    def _():
        state_ref[...] = jnp.zeros(state_ref.shape, F32)

    ti = lax.broadcasted_iota(jnp.int32, (C, C), 0)
    si = lax.broadcasted_iota(jnp.int32, (C, C), 1)
    incl = si <= ti
    strict = si < ti
    tri = incl.astype(F32)
    bf = lambda t: t.astype(BF16)
    dot = lambda x, y: jnp.dot(bf(x), bf(y), preferred_element_type=F32)
    hs = range(hb)
    r, lw, k, v, a, b = ([ref[0, h] for h in hs] for ref in (r_ref, lw_ref, k_ref, v_ref, a_ref, b_ref))
    cum = [jnp.dot(tri, lw[h], preferred_element_type=F32, precision=HIGHEST) for h in hs]
    ar = [bf(jnp.concatenate([a[h] * jnp.exp(cum[h] - lw[h]), r[h] * jnp.exp(cum[h])], axis=0)) for h in hs]
    inv = [jnp.exp(-cum[h]) for h in hs]
    bk = [bf(jnp.concatenate([b[h] * inv[h], k[h] * inv[h]], axis=0)) for h in hs]
    g = [_nt_dot(ar[h], bk[h]) for h in hs]
    s0 = [state_ref[h] for h in hs]
    x = [_nt_dot(ar[h], bf(s0[h])) for h in hs]
    u = [x[h][:C] + dot(jnp.where(strict, g[h][:C, C:], 0.0), v[h]) for h in hs]
    lp = [jnp.where(strict, g[h][:C, :C], 0.0) for h in hs]
    n_double = max(1, (C - 1).bit_length())
    for it in range(n_double):
        u = [u[h] + dot(lp[h], u[h]) for h in hs]
        if it + 1 < n_double:
            lp = [dot(lp[h], lp[h]) for h in hs]
    for h in hs:
        m_rb = jnp.where(incl, g[h][C:, :C], 0.0)
        m_rk = jnp.where(incl, g[h][C:, C:], 0.0)
        y_ref[0, h] = x[h][C:] + dot(m_rb, u[h]) + dot(m_rk, v[h])
    for h in hs:
        to_end = jnp.exp(cum[h][C - 1:C, :] - cum[h])
        uv = jnp.concatenate([u[h], v[h]], axis=0)
        bk_end = jnp.concatenate([b[h] * to_end, k[h] * to_end], axis=0)
        state_ref[h] = s0[h] * jnp.exp(cum[h][C - 1:C, :]) + _tn_dot(bf(uv), bf(bk_end))


def rwkv_scan(r, lw, k, v, a, b):
    B, H, S, N = r.shape
    C = _tile(S, RW_CHUNK)
    hb = _tile(H, 8)
    spec = pl.BlockSpec((1, hb, C, N), lambda bi, hi, ci: (bi, hi, ci, 0))
    return pl.pallas_call(
        functools.partial(_rwkv_chunk_kernel, hb=hb, C=C),
        grid=(B, H // hb, S // C),
        in_specs=[spec] * 6,
        out_specs=spec,
        out_shape=jax.ShapeDtypeStruct((B, H, S, N), F32),
        scratch_shapes=[pltpu.VMEM((hb, N, N), F32)],
        compiler_params=_cparams("parallel", "parallel", "arbitrary"),
    )(r, lw, k, v, a, b)


def _conv_silu(x, tail_ref, w_ref, b_ref):
    L = x.shape[0]
    row = lax.broadcasted_iota(jnp.int32, x.shape, 0)
    tail = tail_ref[...]
    w = w_ref[...]
    acc = b_ref[...] + w[SSM_CONV - 1:SSM_CONV, :] * x
    for s in range(1, SSM_CONV):
        cur = pltpu.roll(x, s, axis=0)
        prev = pltpu.roll(tail, s, axis=0)
        shifted = jnp.where(row < s, jnp.concatenate([prev, cur[8:]], axis=0), cur)
        acc += w[SSM_CONV - 1 - s:SSM_CONV - s, :] * shifted
    tail_ref[...] = x[L - 8:, :]
    return acc * jax.nn.sigmoid(acc)


def _ssd_kernel(xs_ref, bm_ref, cm_ref, z_ref, dt_ref, dtt_ref, a_ref, acol_ref, d_ref, nw_ref, ex_ref,
                wx_ref, wb_ref, wc_ref, bx_ref, bb_ref, bc_ref, o_ref,
                state_ref, tx_ref, tb_ref, tc_ref, *, L, E, P):
    @pl.when(pl.program_id(2) == 0)
    def _():
        state_ref[...] = jnp.zeros(state_ref.shape, F32)
        tx_ref[...] = jnp.zeros(tx_ref.shape, F32)
        tb_ref[...] = jnp.zeros(tb_ref.shape, F32)
        tc_ref[...] = jnp.zeros(tc_ref.shape, F32)

    li = lax.broadcasted_iota(jnp.int32, (L, L), 0)
    si = lax.broadcasted_iota(jnp.int32, (L, L), 1)
    lower = si <= li
    hdot = functools.partial(jnp.dot, preferred_element_type=F32, precision=HIGHEST)
    ex = ex_ref[...]
    a_row = a_ref[0]
    dt = dt_ref[0, 0]
    cs = hdot(lower.astype(F32), dt * a_row)
    cs_row = hdot(dtt_ref[0, 0] * acol_ref[0], (li <= si).astype(F32))
    cs_last = cs[L - 1:L, :]
    xs = _conv_silu(xs_ref[...], tx_ref, wx_ref, bx_ref)
    xdt = xs * hdot(dt, ex)
    bm = _conv_silu(bm_ref[...], tb_ref, wb_ref, bb_ref).astype(BF16)
    cm = _conv_silu(cm_ref[...], tc_ref, wc_ref, bc_ref).astype(BF16)
    cb = _nt_dot(cm, bm)
    state_in = state_ref[...]
    y = hdot(jnp.exp(cs), ex) * jnp.dot(cm, state_in.astype(BF16), preferred_element_type=F32)
    y += d_ref[0] * xs
    lane = lax.broadcasted_iota(jnp.int32, (L, 2 * P), 1)
    ydiag = []
    for pr in range(E // 2):
        xpair = xdt[:, pr * 2 * P:(pr + 1) * 2 * P]
        acc = jnp.zeros((L, 2 * P), F32)
        for sub in range(2):
            e = 2 * pr + sub
            seg = jnp.where(lower, jnp.exp(jnp.minimum(cs[:, e:e + 1] - cs_row[e:e + 1, :], 0.0)), 0.0)
            xm = jnp.where((lane >= sub * P) & (lane < (sub + 1) * P), xpair, 0.0)
            acc += jnp.dot((cb * seg).astype(BF16), xm.astype(BF16), preferred_element_type=F32)
        ydiag.append(acc)
    y += jnp.concatenate(ydiag, axis=1)
    xend = (xdt * hdot(jnp.exp(cs_last - cs), ex)).astype(BF16)
    state_ref[...] = hdot(jnp.exp(cs_last), ex) * state_in + _tn_dot(bm, xend)
    zg = z_ref[...]
    y = y * (zg * jax.nn.sigmoid(zg))
    y = y * lax.rsqrt(jnp.mean(y * y, axis=-1, keepdims=True) + SSM_NORM_EPS)
    o_ref[...] = (y * nw_ref[...]).astype(o_ref.dtype)


def ssd_scan(proj, col_z, col_xbc, conv_w, conv_b, dt, a_neg, d_skip, norm_w, B, S, H):
    G, N, P = SSM_GROUPS, SSM_STATE, HEAD_DIM
    E = H // G
    EP = E * P
    W = H * P
    L = math.gcd(S, SSM_CHUNK)
    nc = S // L
    assert col_z % EP == 0 and col_xbc % EP == 0 and L % 8 == 0
    zb, xb, bb = col_z // EP, col_xbc // EP, (col_xbc + W) // N
    conv_b = conv_b.reshape(1, -1)
    dt_g = dt.reshape(B, S, G, E).transpose(0, 2, 1, 3)
    dt_gt = dt_g.transpose(0, 1, 3, 2)
    ex = (jnp.arange(EP)[None, :] // P == jnp.arange(E)[:, None]).astype(F32)
    d_full = jnp.repeat(d_skip.reshape(G, 1, E), P, axis=2)
    kern = functools.partial(_ssd_kernel, L=L, E=E, P=P)
    return pl.pallas_call(
        kern,
        grid=(B, G, nc),
        in_specs=[pl.BlockSpec((L, EP), lambda b, g, c: (b * nc + c, xb + g)),
                  pl.BlockSpec((L, N), lambda b, g, c: (b * nc + c, bb + g)),
                  pl.BlockSpec((L, N), lambda b, g, c: (b * nc + c, bb + G + g)),
                  pl.BlockSpec((L, EP), lambda b, g, c: (b * nc + c, zb + g)),
                  pl.BlockSpec((1, 1, L, E), lambda b, g, c: (b, g, c, 0)),
                  pl.BlockSpec((1, 1, E, L), lambda b, g, c: (b, g, 0, c)),
                  pl.BlockSpec((1, 1, E), lambda b, g, c: (g, 0, 0)),
                  pl.BlockSpec((1, E, 1), lambda b, g, c: (g, 0, 0)),
                  pl.BlockSpec((1, 1, EP), lambda b, g, c: (g, 0, 0)),
                  pl.BlockSpec((1, EP), lambda b, g, c: (0, g)),
                  pl.BlockSpec((E, EP), lambda b, g, c: (0, 0)),
                  pl.BlockSpec((SSM_CONV, EP), lambda b, g, c: (0, g)),
                  pl.BlockSpec((SSM_CONV, N), lambda b, g, c: (0, W // N + g)),
                  pl.BlockSpec((SSM_CONV, N), lambda b, g, c: (0, W // N + G + g)),
                  pl.BlockSpec((1, EP), lambda b, g, c: (0, g)),
                  pl.BlockSpec((1, N), lambda b, g, c: (0, W // N + g)),
                  pl.BlockSpec((1, N), lambda b, g, c: (0, W // N + G + g))],
        out_specs=pl.BlockSpec((L, EP), lambda b, g, c: (b * nc + c, g)),
        out_shape=jax.ShapeDtypeStruct((B * S, W), BF16),
        scratch_shapes=[pltpu.VMEM((N, EP), F32), pltpu.VMEM((8, EP), F32), pltpu.VMEM((8, N), F32),
                        pltpu.VMEM((8, N), F32)],
        compiler_params=_cparams("parallel", "parallel", "arbitrary"),
    )(proj, proj, proj, proj, dt_g, dt_gt, a_neg.reshape(G, 1, E), a_neg.reshape(G, E, 1), d_full,
      norm_w.reshape(1, W), ex, conv_w, conv_w, conv_w, conv_b, conv_b, conv_b)


def _extract_top(v, n):
    rows = []
    for _ in range(n):
        m = jnp.max(v, axis=0, keepdims=True)
        rows.append(m)
        v = jnp.where(v == m, -jnp.inf, v)
    return rows


def _peer_topk_kernel(q_ref, keys_ref, s1_ref, p1_ref, th_ref, p0_ref, *, half):
    def scores(hp):
        return _nt_dot(keys_ref[hp], q_ref[:, hp * half:(hp + 1) * half].astype(BF16))

    for hd in range(PEER_HEADS):
        rows = slice(hd * PEER_KEYS, (hd + 1) * PEER_KEYS)
        s0, s1 = scores(2 * hd), scores(2 * hd + 1)
        a_rows, b_rows = _extract_top(s0, PEER_TOPK), _extract_top(s1, PEER_TOPK)
        cand = [a_rows[i] + b_rows[j] for i in range(PEER_TOPK) for j in range(PEER_TOPK // (i + 1))]
        pad = (-len(cand)) % 8
        cand = jnp.concatenate(cand + [jnp.full_like(cand[0], -jnp.inf)] * pad, axis=0)
        best = _extract_top(cand, PEER_TOPK)
        tau = best[PEER_TOPK - 1]
        tot = jnp.exp(best[0] - best[0])
        for bi in best[1:]:
            tot += jnp.exp(bi - best[0])
        c = best[0] + jnp.log(tot)
        th = jnp.full_like(s0, jnp.inf)
        for bj in b_rows:
            th = jnp.where(s0 + bj >= tau, bj, th)
        m0 = a_rows[0]
        s1_ref[rows, :] = s1
        p1_ref[rows, :] = jnp.exp(s1 + (m0 - c))
        th_ref[rows, :] = th
        p0_ref[rows, :] = jnp.exp(s0 - m0)


def peer_topk(qp, sub_keys):
    T = qp.shape[0]
    half = sub_keys.shape[-1]
    tm = _tile(T, 512)
    rows = PEER_HEADS * PEER_KEYS
    keys = sub_keys.reshape(2 * PEER_HEADS, PEER_KEYS, half).astype(BF16)
    return pl.pallas_call(
        functools.partial(_peer_topk_kernel, half=half),
        grid=(T // tm,),
        in_specs=[pl.BlockSpec((tm, qp.shape[1]), lambda i: (i, 0)),
                  pl.BlockSpec(keys.shape, lambda i: (0, 0, 0))],
        out_specs=[pl.BlockSpec((rows, tm), lambda i: (0, i))] * 4,
        out_shape=[jax.ShapeDtypeStruct((rows, T), F32)] * 4,
        compiler_params=_cparams("parallel"),
    )(qp, keys)


def _peer_kernel(xn_ref, down_ref, up_ref, s1_ref, p1_ref, th_ref, p0_ref, res_ref, o_ref, w_ref, hid_ref,
                 *, tm, tn, nj):
    j = pl.program_id(1)
    na = tn // PEER_KEYS

    @pl.when(j == 0)
    def _():
        o_ref[...] = res_ref[...]
        w_ref[1] = jnp.zeros(w_ref.shape[1:], w_ref.dtype)

    jt = jnp.minimum(j, nj - 1)

    def run(rd, wr):
        arow = lambda ref, hd, ai: ref[pl.ds(hd * PEER_KEYS + jt * na + ai, 1), :]
        th_rows = [[arow(th_ref, hd, ai) for hd in range(PEER_HEADS)] for ai in range(na)]
        p0_rows = [[arow(p0_ref, hd, ai) for hd in range(PEER_HEADS)] for ai in range(na)]
        o_ref[...] += jnp.dot(w_ref[rd], up_ref[...], preferred_element_type=F32)
        hid_ref[...] = jnp.dot(xn_ref[...], down_ref[...], preferred_element_type=F32)
        for tb in range(tm // LANES):
            tok = slice(tb * LANES, (tb + 1) * LANES)
            for ai in range(na):
                exp_cols = slice(ai * PEER_KEYS, (ai + 1) * PEER_KEYS)
                g = jnp.zeros((PEER_KEYS, LANES), F32)
                for hd in range(PEER_HEADS):
                    keys = slice(hd * PEER_KEYS, (hd + 1) * PEER_KEYS)
                    sel = s1_ref[keys, tok] >= th_rows[ai][hd][:, tok]
                    g += jnp.where(sel, p0_rows[ai][hd][:, tok] * p1_ref[keys, tok], 0.0)
                hid = hid_ref[tok, exp_cols]
                act = 0.5 * hid * (1.0 + lax.erf(hid * (2.0 ** -0.5)))
                w_ref[wr, tok, exp_cols] = (g.T * act).astype(w_ref.dtype)

    @pl.when(j % 2 == 0)
    def _():
        run(1, 0)

    @pl.when(j % 2 == 1)
    def _():
        run(0, 1)


def peer_dense(xn, down_t, up, sel, res):
    T, D = xn.shape
    NE = up.shape[0]
    tm, tn = _tile(T, 512), _tile(NE, 256)
    nj = NE // tn
    once = pl.Buffered(1)
    return pl.pallas_call(
        functools.partial(_peer_kernel, tm=tm, tn=tn, nj=nj),
        grid=(T // tm, nj + 1),
        in_specs=[pl.BlockSpec((tm, D), lambda i, j: (i, 0), pipeline_mode=once),
                  pl.BlockSpec((D, tn), lambda i, j: (0, jnp.minimum(j, nj - 1))),
                  pl.BlockSpec((tn, D), lambda i, j: (jnp.maximum(j - 1, 0), 0)),
                  *[pl.BlockSpec((s.shape[0], tm), lambda i, j: (0, i), pipeline_mode=once) for s in sel],
                  pl.BlockSpec((tm, D), lambda i, j: (i, 0), pipeline_mode=once)],
        out_specs=pl.BlockSpec((tm, D), lambda i, j: (i, 0)),
        out_shape=jax.ShapeDtypeStruct((T, D), F32),
        scratch_shapes=[pltpu.VMEM((2, tm, tn), BF16), pltpu.VMEM((tm, tn), F32)],
        compiler_params=_cparams("parallel", "arbitrary"),
    )(xn, down_t, up, *sel, res)


PROJ_ALIGN = 512


def _proj_weight(w, sizes, offs, total):
    parts, src = [], 0
    for sz, off, nxt in zip(sizes, offs, offs[1:] + [total]):
        seg = w[:, src:src + sz]
        src += sz
        parts.append(jnp.pad(seg, ((0, 0), (0, nxt - off - sz))))
    return jnp.concatenate(parts, axis=1).astype(BF16)


def kernel(x, norm_mix_w, w_in, da_q_norm_w, da_k_norm_w, da_lambda, da_sub_norm_w, rw_mu, rw_w0, rw_w2, rw_a0, rw_a2, rw_g2, rw_k_k, rw_k_a, rw_r_k, rw_ln_w, rw_ln_b, ssm_conv_w, ssm_conv_b, ssm_dt_bias, ssm_a_log, ssm_d, ssm_norm_w, w_branch, w_out, norm_ffn_w, peer_w_query, peer_sub_keys, peer_down, peer_up):
    B, S, D = x.shape
    T = B * S
    depth = w_in.shape[0]
    da_heads = D // 512
    da_cols = 2 * da_heads * HEAD_DIM
    da_width = da_heads * 2 * HEAD_DIM
    rw_heads = D // 256
    rw_width = rw_heads * HEAD_DIM
    rw_sizes = (rw_width, rw_width, rw_width) + RW_LORA
    rw_cols = sum(rw_sizes)
    ssm_heads = D // 128
    ssm_width = ssm_heads * HEAD_DIM
    ssm_xbc = ssm_width + 2 * SSM_GROUPS * SSM_STATE
    sizes = (3 * D, da_cols, da_cols, da_width, rw_cols, ssm_width, ssm_xbc, ssm_heads)
    assert sum(sizes) == w_in.shape[2]
    offs, proj_cols = [], 0
    for sz in sizes:
        offs.append(proj_cols)
        proj_cols += -(-sz // PROJ_ALIGN) * PROJ_ALIGN
    c_q, c_k, c_v, c_rw, c_z, c_xbc, c_dt = offs[1:8]

    pos = jnp.arange(S, dtype=F32)
    inv_freq = ROPE_THETA ** (-jnp.arange(0, HEAD_DIM, 2, dtype=F32) / HEAD_DIM)
    ang = pos[:, None] * inv_freq[None, :]
    cos, sin = jnp.cos(ang), jnp.sin(ang)

    xt = x.reshape(T, D)
    for l in range(depth):
        lambda_init = 0.8 - 0.6 * math.exp(-0.3 * l)
        xn = rmsnorm_bf16(xt, norm_mix_w[l])
        proj = matmul(xn, _proj_weight(w_in[l], sizes, offs, proj_cols))

        q = proj[:, c_q:c_q + da_cols].reshape(B, S, 2 * da_heads, HEAD_DIM)
        k = proj[:, c_k:c_k + da_cols].reshape(B, S, 2 * da_heads, HEAD_DIM)
        q = _rotary(_head_rms(q, da_q_norm_w[l]), cos, sin) * (HEAD_DIM ** -0.5)
        k = _rotary(_head_rms(k, da_k_norm_w[l]), cos, sin)
        v = proj[:, c_v:c_v + da_width].reshape(B, S, da_width)
        lv = da_lambda[l]
        lam = jnp.exp(jnp.sum(lv[0] * lv[1])) - jnp.exp(jnp.sum(lv[2] * lv[3])) + lambda_init
        o_da = diff_attention(lam.reshape(1), q.reshape(B, S, da_cols).astype(BF16),
                              k.reshape(B, S, da_cols).astype(BF16), v.astype(BF16), da_sub_norm_w[l], lambda_init)
        o_da = o_da.reshape(T, da_width)

        p = proj[:, c_rw:c_rw + rw_cols].reshape(B, S, rw_cols)
        p_prev = jnp.pad(p, ((0, 0), (1, 0), (0, 0)))[:, :-1]
        p = p + (p_prev - p) * rw_mu[l]
        r_, k_, v_, w_lo, a_lo, g_lo = jnp.split(p, [sum(rw_sizes[:n]) for n in range(1, len(rw_sizes))], axis=-1)
        w_ = -jax.nn.softplus(-(rw_w0[l] + jnp.tanh(w_lo) @ rw_w2[l])) - 0.5
        logw = -jnp.exp(w_)
        a_ = jax.nn.sigmoid(rw_a0[l] + a_lo @ rw_a2[l])
        g_ = jax.nn.sigmoid(g_lo) @ rw_g2[l]
        heads = lambda t: t.reshape(B, S, rw_heads, HEAD_DIM).transpose(0, 2, 1, 3)
        kk = heads(k_ * rw_k_k[l])
        kk = kk * lax.rsqrt(jnp.maximum(jnp.sum(kk * kk, axis=-1, keepdims=True), 1e-24))
        k_ = k_ * (1.0 + (a_ - 1.0) * rw_k_a[l])
        rh, kh, vh, ah = heads(r_), heads(k_), heads(v_), heads(a_)
        y = rwkv_scan(rh, heads(logw), kh, vh, -kk, kk * ah)
        mean = jnp.mean(y, axis=-1, keepdims=True)
        var = jnp.mean(jnp.square(y - mean), axis=-1, keepdims=True)
        y = (y - mean) * lax.rsqrt(var + RW_GN_EPS)
        y = y * rw_ln_w[l].reshape(1, rw_heads, 1, HEAD_DIM) + rw_ln_b[l].reshape(1, rw_heads, 1, HEAD_DIM)
        y = y + jnp.sum(rh * kh * rw_r_k[l][None, :, None, :], axis=-1, keepdims=True) * vh
        o_rw = (y.transpose(0, 2, 1, 3).reshape(B, S, rw_width) * g_).astype(BF16).reshape(T, rw_width)

        dt = jax.nn.softplus(proj[:, c_dt:c_dt + ssm_heads].reshape(B, S, ssm_heads) + ssm_dt_bias[l])
        o_ssm = ssd_scan(proj, c_z, c_xbc, ssm_conv_w[l], ssm_conv_b[l], dt, -jnp.exp(ssm_a_log[l]), ssm_d[l],
                         ssm_norm_w[l], B, S, ssm_heads)

        wb = w_branch[l].astype(BF16)
        merged = merge_branches(proj, o_da, o_rw, o_ssm, wb[:da_width], wb[da_width:da_width + rw_width],
                                wb[da_width + rw_width:], D)
        xt = matmul(merged, w_out[l].astype(BF16), residual=xt)

        xn2 = rmsnorm_bf16(xt, norm_ffn_w[l])
        qp = matmul(xn2, peer_w_query[l].astype(BF16))
        sel = peer_topk(qp, peer_sub_keys[l])
        xt = peer_dense(xn2, peer_down[l].astype(BF16).T, peer_up[l].astype(BF16), sel, xt)
    return xt.reshape(B, S, D)
```
